```python
import math
import jax, jax.numpy as jnp
from jax import lax
import numpy as np

D_MODEL = 1024
BATCH = 32
SEQ = 2048
DEPTH = 2

HEAD_DIM = 64
A_HEADS = D_MODEL // (2 * HEAD_DIM)
A_WIDTH = A_HEADS * HEAD_DIM
A_BRANCHES = ((128, 1), (512, 4), (2048, 16))
A_BLOCK = 128
B_HEADS = D_MODEL // (4 * HEAD_DIM)
B_QK_WIDTH = 2 * B_HEADS * HEAD_DIM
B_V_WIDTH = B_HEADS * 2 * HEAD_DIM
Q_BLOCK = 128
C_INNER = D_MODEL
C_HEAD_DIM = 64
C_HEADS = C_INNER // C_HEAD_DIM
C_GROUPS = 2
C_STATE = 128
C_CONV = 4
C_CHUNK = 128
D_INNER = D_MODEL
D_HEADS = 4
D_HEAD_DIM = D_INNER // D_HEADS
D_QK_BLOCK = 4
D_CONV = 4
D_CHUNK = 128
FFN_HIDDEN = -(-8 * D_MODEL // (3 * 256)) * 256
RMS_EPS = 1e-6

AB_SIZES = (A_WIDTH, A_WIDTH, A_WIDTH, B_QK_WIDTH, B_QK_WIDTH, B_V_WIDTH)
CD_SIZES = (C_INNER, C_INNER + 2 * C_GROUPS * C_STATE, C_HEADS,
            D_INNER, D_INNER, D_INNER, D_HEADS, D_HEADS)

kernel_name = "hybrid_dilated_diffattn_ssd_mlstm_trunk"


def _split(t, sizes):
    return jnp.split(t, np.cumsum(sizes)[:-1].tolist(), axis=-1)


def rmsnorm(x, w):
    xf = x.astype(jnp.float32)
    y = xf * lax.rsqrt(jnp.mean(xf * xf, axis=-1, keepdims=True) + RMS_EPS)
    return (y * w.astype(jnp.float32)).astype(x.dtype)


def causal_depthwise_conv(x, w, b):
    k, c = w.shape
    y = lax.conv_general_dilated(x, w[:, None, :], window_strides=(1,), padding=[(k - 1, 0)],
                                 dimension_numbers=("NWC", "WIO", "NWC"), feature_group_count=c)
    return y + b


def local_window_attn(q, k, v, n_back):
    lead, n, dh = q.shape[:-2], q.shape[-2], q.shape[-1]
    npad = -(-n // A_BLOCK) * A_BLOCK
    nb = npad // A_BLOCK
    pad = [(0, 0)] * len(lead) + [(0, npad - n), (0, 0)]
    qb, kb, vb = (jnp.pad(t, pad).reshape(lead + (nb, A_BLOCK, dh)) for t in (q, k, v))
    def with_prev(t):
        prev = jnp.concatenate([jnp.zeros_like(t[..., :1, :, :]), t[..., :-1, :, :]], axis=-3)
        return jnp.concatenate([prev, t], axis=-2)
    kk, vv = with_prev(kb), with_prev(vb)
    s = jnp.einsum("...bqd,...bkd->...bqk", qb, kk).astype(jnp.float32) * (HEAD_DIM ** -0.5)
    qpos = jnp.arange(nb)[:, None, None] * A_BLOCK + jnp.arange(A_BLOCK)[None, :, None]
    kpos = (jnp.arange(nb)[:, None, None] - 1) * A_BLOCK + jnp.arange(2 * A_BLOCK)[None, None, :]
    dist = qpos - kpos
    valid = (dist >= 0) & (dist <= n_back) & (kpos >= 0)
    s = jnp.where(valid, s, -jnp.inf)
    m = jnp.max(s, axis=-1, keepdims=True)
    p = jnp.exp(s - m)
    den = jnp.sum(p, axis=-1, keepdims=True)
    o = jnp.einsum("...bqk,...bkd->...bqd", (p / den).astype(v.dtype), vv)
    lse = (m + jnp.log(den))[..., 0]
    o = o.reshape(lead + (npad, dh))[..., :n, :]
    lse = lse.reshape(lead + (npad,))[..., :n]
    return o, lse


def dilated_branch(q, k, v, window, dilation):
    bn, h, t, dh = q.shape
    def strided(a):
        return a.reshape(bn, h, t // dilation, dilation, dh).swapaxes(2, 3)
    o, lse = local_window_attn(strided(q), strided(k), strided(v), window // dilation)
    return o.swapaxes(2, 3).reshape(bn, h, t, dh), lse.swapaxes(2, 3).reshape(bn, h, t)


def dilated_mixture_attention(q, k, v):
    outs, lses = [], []
    for window, dilation in A_BRANCHES:
        o, lse = dilated_branch(q, k, v, window, dilation)
        outs.append(o)
        lses.append(lse)
    wts = jax.nn.softmax(jnp.stack(lses, axis=0), axis=0)
    return jnp.einsum("gbht,gbhtd->bhtd", wts.astype(q.dtype), jnp.stack(outs, axis=0))


def diff_attention(q, k, v, lam):
    bn, h, t = q.shape[:3]
    nb = t // Q_BLOCK
    kidx = jnp.arange(t)
    def block(i):
        qs = lax.dynamic_slice_in_dim(q, i * Q_BLOCK, Q_BLOCK, axis=2)
        s = jnp.einsum("bhqcd,bhkcd->bchqk", qs, k).astype(jnp.float32) * (HEAD_DIM ** -0.5)
        qpos = i * Q_BLOCK + jnp.arange(Q_BLOCK)
        s = jnp.where(kidx[None, :] <= qpos[:, None], s, -jnp.inf)
        p = jax.nn.softmax(s, axis=-1)
        a = p[:, 0] - lam * p[:, 1]
        return jnp.einsum("bhqk,bhkd->bhqd", a.astype(v.dtype), v)
    out = lax.map(block, jnp.arange(nb))
    return out.transpose(1, 2, 0, 3, 4).reshape(bn, h, t, v.shape[-1])


def attn_mix(hn, w_in, w_out, lq1, lk1, lq2, lk2, subln_w, lambda_init):
    bn, t, _ = hn.shape
    qa, ka, va, qd, kd, vd = _split(hn @ w_in, AB_SIZES)
    def heads(a, n):
        return a.reshape(bn, t, n, -1).transpose(0, 2, 1, 3)
    ya = dilated_mixture_attention(heads(qa, A_HEADS), heads(ka, A_HEADS), heads(va, A_HEADS))
    ya = ya.transpose(0, 2, 1, 3).reshape(bn, t, A_WIDTH)
    lam = jnp.exp(jnp.sum(lq1 * lk1)) - jnp.exp(jnp.sum(lq2 * lk2)) + lambda_init
    qd = qd.reshape(bn, t, B_HEADS, 2, HEAD_DIM).transpose(0, 2, 1, 3, 4)
    kd = kd.reshape(bn, t, B_HEADS, 2, HEAD_DIM).transpose(0, 2, 1, 3, 4)
    yb = diff_attention(qd, kd, heads(vd, B_HEADS), lam)
    yb = rmsnorm(yb, subln_w) * (1.0 - lambda_init)
    yb = yb.transpose(0, 2, 1, 3).reshape(bn, t, B_V_WIDTH)
    return jnp.concatenate([ya, yb], axis=-1) @ w_out


def ssd_chunked(x, dt, a_neg, bm, cm):
    bn, t, h, p = x.shape
    g, n = bm.shape[2:]
    e = h // g
    nc = t // C_CHUNK
    x = x.reshape(bn, nc, C_CHUNK, g, e, p)
    dt = dt.reshape(bn, nc, C_CHUNK, g, e)
    bm = bm.reshape(bn, nc, C_CHUNK, g, n)
    cm = cm.reshape(bn, nc, C_CHUNK, g, n)
    xdt = x * dt[..., None]
    acum = jnp.cumsum(jnp.moveaxis(dt.astype(jnp.float32) * a_neg.reshape(g, e), 2, -1), axis=-1)
    causal = jnp.tril(jnp.ones((C_CHUNK, C_CHUNK), dtype=bool))
    decay = jnp.exp(jnp.where(causal, acum[..., :, None] - acum[..., None, :], -jnp.inf))
    cb = jnp.einsum("bclgn,bcsgn->bcgls", cm, bm)
    y_diag = jnp.einsum("bcgls,bcgels,bcsgep->bclgep", cb, decay, xdt)
    decay_end = jnp.exp(acum[..., -1:] - acum)
    states = jnp.einsum("bcsgn,bcges,bcsgep->bcgepn", bm, decay_end, xdt)
    chunk_decay = jnp.exp(acum[..., -1])
    def step(s, inp):
        st, dec = inp
        return dec[..., None, None] * s + st, s
    s0 = jnp.zeros(states.shape[:1] + states.shape[2:], states.dtype)
    _, s_in = lax.scan(step, s0, (jnp.moveaxis(states, 1, 0), jnp.moveaxis(chunk_decay, 1, 0)))
    s_in = jnp.moveaxis(s_in, 0, 1)
    y_off = jnp.einsum("bclgn,bcgepn,bcgel->bclgep", cm, s_in, jnp.exp(acum))
    return (y_diag + y_off).reshape(bn, t, h, p)


def mlstm_chunkwise(q, k, v, ig, logf):
    bn, t, h, dk = q.shape
    dv = v.shape[-1]
    nc = t // D_CHUNK
    q = q.reshape(bn, nc, D_CHUNK, h, dk)
    k = k.reshape(bn, nc, D_CHUNK, h, dk)
    v = v.reshape(bn, nc, D_CHUNK, h, dv)
    ig = jnp.moveaxis(ig.astype(jnp.float32).reshape(bn, nc, D_CHUNK, h), 2, -1)
    bcum = jnp.cumsum(jnp.moveaxis(logf.astype(jnp.float32).reshape(bn, nc, D_CHUNK, h), 2, -1), axis=-1)
    gtot = bcum[..., -1]
    causal = jnp.tril(jnp.ones((D_CHUNK, D_CHUNK), dtype=bool))
    dmat = jnp.where(causal, bcum[..., :, None] - bcum[..., None, :] + ig[..., None, :], -jnp.inf)
    w_end = gtot[..., None] - bcum + ig
    a_loc = jnp.max(w_end, axis=-1)
    e_end = jnp.exp(w_end - a_loc[..., None])
    c_loc = jnp.einsum("bchs,bcshv,bcshk->bchvk", e_end, v, k)
    n_loc = jnp.einsum("bchs,bcshk->bchk", e_end, k)
    def step(carry, inp):
        cs, ns, ms = carry
        cl, nl, al, gc = inp
        m_new = jnp.maximum(gc + ms, al)
        d_old = jnp.exp(gc + ms - m_new)
        d_loc = jnp.exp(al - m_new)
        c_new = d_old[..., None, None] * cs + d_loc[..., None, None] * cl
        n_new = d_old[..., None] * ns + d_loc[..., None] * nl
        return (c_new, n_new, m_new), (cs, ns, ms)
    init = (jnp.zeros((bn, h, dv, dk), c_loc.dtype), jnp.zeros((bn, h, dk), n_loc.dtype),
            jnp.zeros((bn, h), gtot.dtype))
    xs = tuple(jnp.moveaxis(a, 1, 0) for a in (c_loc, n_loc, a_loc, gtot))
    _, (c_in, n_in, m_in) = lax.scan(step, init, xs)
    c_in, n_in, m_in = (jnp.moveaxis(a, 0, 1) for a in (c_in, n_in, m_in))
    inter_log = bcum + m_in[..., None]
    m_out = jnp.maximum(jnp.max(dmat, axis=-1), inter_log)
    wqk = jnp.exp(dmat - m_out[..., None]) * jnp.einsum("bclhk,bcshk->bchls", q, k)
    e_inter = jnp.exp(inter_log - m_out)
    num = (jnp.einsum("bchls,bcshv->bclhv", wqk, v)
           + jnp.einsum("bclhk,bchvk->bclhv", q, c_in) * jnp.swapaxes(e_inter, 2, 3)[..., None])
    den = jnp.sum(wqk, axis=-1) + jnp.einsum("bclhk,bchk->bchl", q, n_in) * e_inter
    denom = jnp.maximum(jnp.abs(den), jnp.exp(-m_out))
    return (num / jnp.swapaxes(denom, 2, 3)[..., None]).reshape(bn, t, h, dv)


def headwise(x, w):
    bn, t, _ = x.shape
    y = jnp.einsum("btnj,nji->btni", x.reshape(bn, t, w.shape[0], D_QK_BLOCK), w)
    return y.reshape(bn, t, D_INNER)


def ssm_mix(hn, w_in, w_out, c_conv_w, c_conv_b, c_dt_bias, c_a_log, c_d_skip, c_norm_w,
            d_conv_w, d_conv_b, d_wq, d_wk, d_i_bias, d_f_bias, d_norm_w):
    bn, t, _ = hn.shape
    z, xbc, dt_raw, u, v, o_pre, i_pre, f_pre = _split(hn @ w_in, CD_SIZES)
    xbc = jax.nn.silu(causal_depthwise_conv(xbc, c_conv_w, c_conv_b))
    xs, bm, cm = _split(xbc, (C_INNER, C_GROUPS * C_STATE, C_GROUPS * C_STATE))
    dt = jax.nn.softplus((dt_raw + c_dt_bias).astype(jnp.float32))
    a_neg = -jnp.exp(c_a_log.astype(jnp.float32))
    xs_h = xs.reshape(bn, t, C_HEADS, C_HEAD_DIM)
    y = ssd_chunked(xs_h, dt, a_neg, bm.reshape(bn, t, C_GROUPS, C_STATE), cm.reshape(bn, t, C_GROUPS, C_STATE))
    y = (y + xs_h * c_d_skip[:, None]).astype(hn.dtype).reshape(bn, t, C_INNER)
    yc = rmsnorm(y * jax.nn.silu(z), c_norm_w)
    uc = jax.nn.silu(causal_depthwise_conv(u, d_conv_w, d_conv_b))
    q = headwise(uc, d_wq).reshape(bn, t, D_HEADS, D_HEAD_DIM)
    k = headwise(uc, d_wk).reshape(bn, t, D_HEADS, D_HEAD_DIM) * (D_HEAD_DIM ** -0.5)
    vh = v.reshape(bn, t, D_HEADS, D_HEAD_DIM)
    logf = jax.nn.log_sigmoid((f_pre + d_f_bias).astype(jnp.float32))
    hd = mlstm_chunkwise(q, k, vh, i_pre + d_i_bias, logf).astype(hn.dtype)
    hd = rmsnorm(hd, d_norm_w.reshape(D_HEADS, D_HEAD_DIM)).reshape(bn, t, D_INNER)
    yd = jax.nn.sigmoid(o_pre) * hd
    return jnp.concatenate([yc, yd], axis=-1) @ w_out


def swiglu(h, w_gate_up, w_down):
    g, u = jnp.split(h @ w_gate_up, 2, axis=-1)
    return (jax.nn.silu(g) * u) @ w_down


def setup_inputs(seed: int = 0) -> dict:
    key = jax.random.key(seed)
    ks = iter(jax.random.split(key, 40))
    def nrm(shape, scale):
        return jax.random.normal(next(ks), shape, jnp.float32) * scale
    ne = (DEPTH + 1) // 2
    no = DEPTH // 2
    ab_cols = sum(AB_SIZES)
    cd_cols = sum(CD_SIZES)
    xbc_w = C_INNER + 2 * C_GROUPS * C_STATE
    x = nrm((BATCH, SEQ, D_MODEL), 1.0)
    mix_norm_w = 1.0 + nrm((DEPTH, D_MODEL), 0.02)
    ffn_norm_w = 1.0 + nrm((DEPTH, D_MODEL), 0.02)
    ab_w_in = nrm((ne, D_MODEL, ab_cols), D_MODEL ** -0.5)
    ab_w_out = nrm((ne, A_WIDTH + B_V_WIDTH, D_MODEL), (A_WIDTH + B_V_WIDTH) ** -0.5)
    diff_lq1 = nrm((ne, HEAD_DIM), 0.1)
    diff_lk1 = nrm((ne, HEAD_DIM), 0.1)
    diff_lq2 = nrm((ne, HEAD_DIM), 0.1)
    diff_lk2 = nrm((ne, HEAD_DIM), 0.1)
    diff_subln_w = 1.0 + nrm((ne, 2 * HEAD_DIM), 0.02)
    cd_w_in = nrm((no, D_MODEL, cd_cols), D_MODEL ** -0.5)
    c_conv_w = nrm((no, C_CONV, xbc_w), C_CONV ** -0.5)
    c_conv_b = nrm((no, xbc_w), 0.02)
    dt0 = jnp.exp(jax.random.uniform(next(ks), (no, C_HEADS), jnp.float32,
                                     minval=math.log(1e-3), maxval=math.log(1e-1)))
    c_dt_bias = dt0 + jnp.log(-jnp.expm1(-dt0))
    c_a_log = jnp.log(jax.random.uniform(next(ks), (no, C_HEADS), jnp.float32, minval=1.0, maxval=16.0))
    c_d_skip = 1.0 + nrm((no, C_HEADS), 0.1)
    c_norm_w = 1.0 + nrm((no, C_INNER), 0.02)
    d_conv_w = nrm((no, D_CONV, D_INNER), D_CONV ** -0.5)
    d_conv_b = nrm((no, D_INNER), 0.02)
    d_wq = nrm((no, D_INNER // D_QK_BLOCK, D_QK_BLOCK, D_QK_BLOCK), D_QK_BLOCK ** -0.5)
    d_wk = nrm((no, D_INNER // D_QK_BLOCK, D_QK_BLOCK, D_QK_BLOCK), D_QK_BLOCK ** -0.5)
    d_i_bias = nrm((no, D_HEADS), 0.1)
    d_f_bias = jnp.linspace(3.0, 6.0, D_HEADS, dtype=jnp.float32)[None, :] + nrm((no, D_HEADS), 0.1)
    d_norm_w = 1.0 + nrm((no, D_INNER), 0.02)
    cd_w_out = nrm((no, C_INNER + D_INNER, D_MODEL), (C_INNER + D_INNER) ** -0.5)
    ffn_w_gate_up = nrm((DEPTH, D_MODEL, 2 * FFN_HIDDEN), D_MODEL ** -0.5)
    ffn_w_down = nrm((DEPTH, FFN_HIDDEN, D_MODEL), FFN_HIDDEN ** -0.5)
    final_norm_w = 1.0 + nrm((D_MODEL,), 0.02)
    return {"x": x, "mix_norm_w": mix_norm_w, "ffn_norm_w": ffn_norm_w,
            "ab_w_in": ab_w_in, "ab_w_out": ab_w_out,
            "diff_lq1": diff_lq1, "diff_lk1": diff_lk1, "diff_lq2": diff_lq2, "diff_lk2": diff_lk2,
            "diff_subln_w": diff_subln_w, "cd_w_in": cd_w_in,
            "c_conv_w": c_conv_w, "c_conv_b": c_conv_b, "c_dt_bias": c_dt_bias, "c_a_log": c_a_log,
            "c_d_skip": c_d_skip, "c_norm_w": c_norm_w,
            "d_conv_w": d_conv_w, "d_conv_b": d_conv_b, "d_wq": d_wq, "d_wk": d_wk,
            "d_i_bias": d_i_bias, "d_f_bias": d_f_bias, "d_norm_w": d_norm_w, "cd_w_out": cd_w_out,
            "ffn_w_gate_up": ffn_w_gate_up, "ffn_w_down": ffn_w_down, "final_norm_w": final_norm_w}


def reference(x, mix_norm_w, ffn_norm_w, ab_w_in, ab_w_out, diff_lq1, diff_lk1, diff_lq2, diff_lk2,
              diff_subln_w, cd_w_in, c_conv_w, c_conv_b, c_dt_bias, c_a_log, c_d_skip, c_norm_w,
              d_conv_w, d_conv_b, d_wq, d_wk, d_i_bias, d_f_bias, d_norm_w, cd_w_out,
              ffn_w_gate_up, ffn_w_down, final_norm_w):
    h = x
    for layer in range(DEPTH):
        j = layer // 2
        hn = rmsnorm(h, mix_norm_w[layer])
        if layer % 2 == 0:
            lambda_init = 0.8 - 0.6 * math.exp(-0.3 * layer)
            h = h + attn_mix(hn, ab_w_in[j], ab_w_out[j], diff_lq1[j], diff_lk1[j], diff_lq2[j],
                             diff_lk2[j], diff_subln_w[j], lambda_init)
        else:
            h = h + ssm_mix(hn, cd_w_in[j], cd_w_out[j], c_conv_w[j], c_conv_b[j], c_dt_bias[j],
                            c_a_log[j], c_d_skip[j], c_norm_w[j], d_conv_w[j], d_conv_b[j],
                            d_wq[j], d_wk[j], d_i_bias[j], d_f_bias[j], d_norm_w[j])
        h = h + swiglu(rmsnorm(h, ffn_norm_w[layer]), ffn_w_gate_up[layer], ffn_w_down[layer])
    return rmsnorm(h, final_norm_w)
```

```python
import functools
import math

import jax
import jax.numpy as jnp
from jax import lax
from jax.experimental import pallas as pl
from jax.experimental.pallas import tpu as pltpu

F32 = jnp.float32
BF16 = jnp.bfloat16

RMS_EPS = 1e-6
HEAD_DIM = 64
LANES = 128
CHUNK = 128
A_DILATIONS = (1, 4, 16)
NEG = -1e30
VMEM_LIMIT = 56 * 1024 * 1024
ROW_TILE = 512
PROJ_COLS = 512
FFN_COLS = 768
B_TQ = 256

_NT = (((1,), (1,)), ((), ()))
_TN = (((0,), (0,)), ((), ()))


def _cparams(sem):
    return pltpu.CompilerParams(dimension_semantics=sem, vmem_limit_bytes=VMEM_LIMIT)


def _resident(shape):
    zeros = (0,) * len(shape)
    return pl.BlockSpec(shape, lambda *_: zeros, pipeline_mode=pl.Buffered(1))


def _rms(x, w):
    return x * lax.rsqrt(jnp.mean(x * x, axis=-1, keepdims=True) + RMS_EPS) * w


def _sigmoid(x):
    return 1.0 / (1.0 + jnp.exp(-x))


def _silu(x):
    return x * _sigmoid(x)


def _softplus(x):
    return jnp.maximum(x, 0.0) + jnp.log(1.0 + jnp.exp(-jnp.abs(x)))


def _tri(n, lower):
    r = lax.broadcasted_iota(jnp.int32, (n, n), 0)
    c = lax.broadcasted_iota(jnp.int32, (n, n), 1)
    return jnp.where((c <= r) if lower else (r <= c), 1.0, 0.0).astype(F32)


def _norm_proj_kernel(*refs, n_w, has_t):
    x_ref, nw_ref = refs[0], refs[1]
    w_refs = refs[2:2 + n_w]
    pos = 2 + n_w
    wt_ref = refs[pos] if has_t else None
    pos += int(has_t)
    o_refs = refs[pos:pos + n_w]
    ot_ref = refs[pos + n_w] if has_t else None

    xn = _rms(x_ref[...], nw_ref[...]).astype(BF16)
    for w_ref, o_ref in zip(w_refs, o_refs):
        n = w_ref.shape[1]
        for c0 in range(0, n, PROJ_COLS):
            c1 = min(n, c0 + PROJ_COLS)
            o_ref[:, c0:c1] = jnp.dot(xn, w_ref[:, c0:c1],
                                      preferred_element_type=F32).astype(o_ref.dtype)
    if has_t:
        ot_ref[...] = lax.dot_general(wt_ref[...], xn, _NT, preferred_element_type=F32)


def _norm_proj(x, norm_w, weights, out_dtypes, w_t=None):
    n_rows, d = x.shape
    tm = min(ROW_TILE, n_rows)
    has_t = w_t is not None
    in_specs = [pl.BlockSpec((tm, d), lambda i: (i, 0)), _resident((1, d))]
    in_specs += [_resident(w.shape) for w in weights]
    args = [x, norm_w.reshape(1, d)] + list(weights)
    out_shape = [jax.ShapeDtypeStruct((n_rows, w.shape[1]), dt) for w, dt in zip(weights, out_dtypes)]
    out_specs = [pl.BlockSpec((tm, w.shape[1]), lambda i: (i, 0)) for w in weights]
    if has_t:
        in_specs.append(_resident(w_t.shape))
        args.append(w_t)
        out_shape.append(jax.ShapeDtypeStruct((w_t.shape[0], n_rows), F32))
        out_specs.append(pl.BlockSpec((w_t.shape[0], tm), lambda i: (0, i)))
    return pl.pallas_call(
        functools.partial(_norm_proj_kernel, n_w=len(weights), has_t=has_t),
        grid=(n_rows // tm,),
        in_specs=in_specs, out_specs=out_specs, out_shape=out_shape,
        compiler_params=_cparams(("parallel",)),
        name="norm_proj",
    )(*args)


def _post_kernel(*refs, n_y, final):
    h_ref = refs[0]
    y_refs = refs[1:1 + n_y]
    wo_refs = refs[1 + n_y:1 + 2 * n_y]
    nw_ref, wg_ref, wu_ref, wd_ref = refs[1 + 2 * n_y:5 + 2 * n_y]
    pos = 5 + 2 * n_y
    fw_ref = refs[pos] if final else None
    pos += int(final)
    o_ref, hid_ref = refs[pos], refs[pos + 1]

    h1 = h_ref[...]
    for y_ref, wo_ref in zip(y_refs, wo_refs):
        h1 = h1 + jnp.dot(y_ref[...], wo_ref[...], preferred_element_type=F32)
    xn = _rms(h1, nw_ref[...]).astype(BF16)
    f = wg_ref.shape[1]
    for c0 in range(0, f, FFN_COLS):
        c1 = min(f, c0 + FFN_COLS)
        g = jnp.dot(xn, wg_ref[:, c0:c1], preferred_element_type=F32)
        u = jnp.dot(xn, wu_ref[:, c0:c1], preferred_element_type=F32)
        hid_ref[:, c0:c1] = (_silu(g) * u).astype(BF16)
    h2 = h1 + jnp.dot(hid_ref[...], wd_ref[...], preferred_element_type=F32)
    if final:
        h2 = _rms(h2, fw_ref[...])
    o_ref[...] = h2


def _post(h, ys, wos, ffn_norm_w, wg, wu, wd, final_w=None):
    n_rows, d = h.shape
    tm = min(ROW_TILE, n_rows)
    final = final_w is not None
    in_specs = [pl.BlockSpec((tm, d), lambda i: (i, 0))]
    in_specs += [pl.BlockSpec((tm, y.shape[1]), lambda i: (i, 0)) for y in ys]
    in_specs += [_resident(w.shape) for w in wos]
    in_specs += [_resident((1, d)), _resident(wg.shape), _resident(wu.shape), _resident(wd.shape)]
    args = [h] + list(ys) + list(wos) + [ffn_norm_w.reshape(1, d), wg, wu, wd]
    if final:
        in_specs.append(_resident((1, d)))
        args.append(final_w.reshape(1, d))
    return pl.pallas_call(
        functools.partial(_post_kernel, n_y=len(ys), final=final),
        grid=(n_rows // tm,),
        in_specs=in_specs,
        out_specs=pl.BlockSpec((tm, d), lambda i: (i, 0)),
        out_shape=jax.ShapeDtypeStruct((n_rows, d), F32),
        scratch_shapes=[pltpu.VMEM((tm, wg.shape[1]), BF16)],
        compiler_params=_cparams(("parallel",)),
        name="post_ffn",
    )(*args)


def _mixer_a_kernel(q_ref, k_ref, v_ref, o_ref, qf, kf, vf, o0, o1, o2, l0, l1, l2, *, seq):
    qf[...] = q_ref[...].astype(F32)
    kf[...] = k_ref[...].astype(F32)
    vf[...] = v_ref[...].astype(F32)
    o_scr, l_scr = (o0, o1, o2), (l0, l1, l2)

    lo = lax.broadcasted_iota(jnp.int32, (CHUNK, LANES), 1) < HEAD_DIM
    r1 = lax.broadcasted_iota(jnp.int32, (CHUNK, CHUNK), 0)
    c1 = lax.broadcasted_iota(jnp.int32, (CHUNK, CHUNK), 1)
    causal_bias = jnp.where(c1 <= r1, 0.0, NEG).astype(F32)
    r2 = lax.broadcasted_iota(jnp.int32, (CHUNK, 2 * CHUNK), 0)
    c2 = lax.broadcasted_iota(jnp.int32, (CHUNK, 2 * CHUNK), 1)
    band_bias = jnp.where((c2 >= r2) & (c2 <= r2 + CHUNK), 0.0, NEG).astype(F32)

    def rows(start, n, d):
        return pl.ds(start, n) if d == 1 else pl.ds(start, n, stride=d)

    def attend(q, k, v, bias):
        kb, vb = k.astype(BF16), v.astype(BF16)
        outs, lses = [], []
        for h in range(2):
            qh = jnp.where(lo if h == 0 else jnp.logical_not(lo), q, 0.0).astype(BF16)
            s = lax.dot_general(qh, kb, _NT, preferred_element_type=F32) + bias
            m = jnp.max(s, axis=-1, keepdims=True)
            p = jnp.exp(s - m)
            den = jnp.sum(p, axis=-1, keepdims=True)
            pv = jnp.dot(p.astype(BF16), vb, preferred_element_type=F32)
            outs.append(pv / den)
            lses.append(m + jnp.log(den))
        return jnp.where(lo, outs[0], outs[1]), jnp.where(lo, lses[0], lses[1])

    for g, d in enumerate(A_DILATIONS):
        nb = seq // (d * CHUNK)

        def first_block(r, g=g, d=d):
            sl = rows(r, CHUNK, d)
            o, lse = attend(qf[sl, :], kf[sl, :], vf[sl, :], causal_bias)
            o_scr[g][sl, :] = o
            l_scr[g][sl, :] = lse

        def later_block(i, r, g=g, d=d):
            qs = rows(i * (CHUNK * d) + r, CHUNK, d)
            ks = rows((i - 1) * (CHUNK * d) + r, 2 * CHUNK, d)
            o, lse = attend(qf[qs, :], kf[ks, :], vf[ks, :], band_bias)
            o_scr[g][qs, :] = o
            l_scr[g][qs, :] = lse

        def subsequence(r, carry, nb=nb, first_block=first_block, later_block=later_block):
            first_block(r)
            if nb > 1:
                lax.fori_loop(1, nb, lambda i, c: (later_block(i, r), c)[1], 0)
            return carry

        if d == 1:
            subsequence(0, 0)
        else:
            lax.fori_loop(0, d, subsequence, 0)

    def merge(c, carry):
        sl = pl.ds(pl.multiple_of(c * (2 * CHUNK), 2 * CHUNK), 2 * CHUNK)
        ls = [l[sl, :] for l in l_scr]
        mx = jnp.maximum(jnp.maximum(ls[0], ls[1]), ls[2])
        es = [jnp.exp(l - mx) for l in ls]
        num = es[0] * o_scr[0][sl, :] + es[1] * o_scr[1][sl, :] + es[2] * o_scr[2][sl, :]
        o_ref[sl, :] = (num / (es[0] + es[1] + es[2])).astype(o_ref.dtype)
        return carry

    lax.fori_loop(0, seq // (2 * CHUNK), merge, 0)


def _mixer_a(qkv, batch, seq, a_width):
    n_pairs = a_width // LANES
    blk = lambda off: pl.BlockSpec((seq, LANES), lambda b, p: (b, off + p))
    return pl.pallas_call(
        functools.partial(_mixer_a_kernel, seq=seq),
        grid=(batch, n_pairs),
        in_specs=[blk(0), blk(n_pairs), blk(2 * n_pairs)],
        out_specs=pl.BlockSpec((seq, LANES), lambda b, p: (b, p)),
        out_shape=jax.ShapeDtypeStruct((batch * seq, a_width), BF16),
        scratch_shapes=[pltpu.VMEM((seq, LANES), F32)] * 9,
        compiler_params=_cparams(("parallel", "parallel")),
        name="mixer_a",
    )(qkv, qkv, qkv)


def _mixer_b_kernel(q_ref, k_ref, v_ref, lq1, lk1, lq2, lk2, sw_ref, o_ref, *, seq, lambda_init):
    tq = B_TQ
    lam = (jnp.exp(jnp.sum(lq1[...] * lk1[...], axis=-1, keepdims=True))
           - jnp.exp(jnp.sum(lq2[...] * lk2[...], axis=-1, keepdims=True)) + lambda_init)
    lo = lax.broadcasted_iota(jnp.int32, (tq, LANES), 1) < HEAD_DIM
    ri = lax.broadcasted_iota(jnp.int32, (tq, tq), 0)
    ci = lax.broadcasted_iota(jnp.int32, (tq, tq), 1)
    diag_bias = jnp.where(ci <= ri, 0.0, NEG).astype(F32)
    zero = jnp.zeros((), BF16)

    def q_block(i, carry):
        q = q_ref[pl.ds(pl.multiple_of(i * tq, tq), tq), :]
        qs = (jnp.where(lo, q, zero), jnp.where(lo, zero, q))

        def kv_step(j, state, bias):
            ks = pl.ds(pl.multiple_of(j * tq, tq), tq)
            k, v = k_ref[ks, :], v_ref[ks, :]
            new = []
            for c in range(2):
                m, l, acc = state[3 * c:3 * c + 3]
                s = lax.dot_general(qs[c], k, _NT, preferred_element_type=F32)
                if bias is not None:
                    s = s + bias
                m_new = jnp.maximum(m, jnp.max(s, axis=-1, keepdims=True))
                alpha = jnp.exp(m - m_new)
                p = jnp.exp(s - m_new)
                l = alpha * l + jnp.sum(p, axis=-1, keepdims=True)
                acc = alpha * acc + jnp.dot(p.astype(BF16), v, preferred_element_type=F32)
                new += [m_new, l, acc]
            return tuple(new)

        init = (jnp.full((tq, 1), NEG, F32), jnp.zeros((tq, 1), F32), jnp.zeros((tq, LANES), F32)) * 2
        state = lax.fori_loop(0, i, lambda j, st: kv_step(j, st, None), init)
        _, la, acc_a, _, lb, acc_b = kv_step(i, state, diag_bias)
        y = acc_a / la - lam * (acc_b / lb)
        y = _rms(y, sw_ref[...]) * (1.0 - lambda_init)
        o_ref[pl.ds(pl.multiple_of(i * tq, tq), tq), :] = y.astype(o_ref.dtype)
        return carry

    lax.fori_loop(0, seq // tq, q_block, 0)


def _mixer_b(qkv, lq1, lk1, lq2, lk2, subln_w, batch, seq, col0, n_heads, lambda_init):
    off = col0 // LANES
    blk = lambda o: pl.BlockSpec((seq, LANES), lambda b, h: (b, o + h))
    vec = lambda n: pl.BlockSpec((1, n), lambda b, h: (0, 0))
    return pl.pallas_call(
        functools.partial(_mixer_b_kernel, seq=seq, lambda_init=lambda_init),
        grid=(batch, n_heads),
        in_specs=[blk(off), blk(off + n_heads), blk(off + 2 * n_heads),
                  vec(HEAD_DIM), vec(HEAD_DIM), vec(HEAD_DIM), vec(HEAD_DIM), vec(LANES)],
        out_specs=pl.BlockSpec((seq, LANES), lambda b, h: (b, h)),
        out_shape=jax.ShapeDtypeStruct((batch * seq, n_heads * LANES), BF16),
        compiler_params=_cparams(("parallel", "parallel")),
        name="mixer_b",
    )(qkv, qkv, qkv, lq1.reshape(1, -1), lk1.reshape(1, -1), lq2.reshape(1, -1),
      lk2.reshape(1, -1), subln_w.reshape(1, -1))


CONV_K = 4
HALO = 16


def _conv_silu_chunk(x_ref, win, cw_ref, cb_ref, c):
    t0 = pl.multiple_of(c * CHUNK, CHUNK)
    prev = x_ref[pl.ds(pl.multiple_of(jnp.maximum(t0 - HALO, 0), HALO), HALO), :].astype(F32)
    win[0:HALO, :] = jnp.where(c > 0, prev, 0.0)
    win[HALO:HALO + CHUNK, :] = x_ref[pl.ds(t0, CHUNK), :].astype(F32)
    acc = cb_ref[...] + cw_ref[CONV_K - 1:CONV_K, :] * win[HALO:HALO + CHUNK, :]
    for j in range(CONV_K - 1):
        s = HALO - (CONV_K - 1) + j
        acc = acc + cw_ref[j:j + 1, :] * win[s:s + CHUNK, :]
    return _silu(acc)


def _ssd_kernel(xbc_ref, z_ref, g_ref, gt_ref, cw_ref, cb_ref, dtb_ref, dtbt_ref, an_ref, ant_ref,
                dsk_ref, nw_ref, o_ref, win, state, *, seq, inner, n_heads, n_groups, d_state):
    hp = inner // n_heads
    pairs_per_group = n_heads // n_groups // 2
    state[...] = jnp.zeros_like(state)
    lo = lax.broadcasted_iota(jnp.int32, (CHUNK, LANES), 1) < hp
    ltri, utri = _tri(CHUNK, True), _tri(CHUNK, False)
    causal = lax.broadcasted_iota(jnp.int32, (CHUNK, CHUNK), 1) <= lax.broadcasted_iota(jnp.int32, (CHUNK, CHUNK), 0)

    def chunk(c, carry):
        t0 = pl.multiple_of(c * CHUNK, CHUNK)
        xa = _conv_silu_chunk(xbc_ref, win, cw_ref, cb_ref, c)
        dt = _softplus(g_ref[pl.ds(t0, CHUNK), :] + dtb_ref[...])
        dt_t = _softplus(gt_ref[0:n_heads, pl.ds(t0, CHUNK)] + dtbt_ref[...])
        acum = jnp.dot(ltri, dt * an_ref[...], preferred_element_type=F32,
                       precision=lax.Precision.HIGHEST)
        acum_t = jnp.dot(dt_t * ant_ref[...], utri, preferred_element_type=F32,
                         precision=lax.Precision.HIGHEST)
        ys = []
        for g in range(n_groups):
            bm = xa[:, inner + g * d_state:inner + (g + 1) * d_state]
            cm = xa[:, inner + (n_groups + g) * d_state:inner + (n_groups + g + 1) * d_state]
            bmb, cmb = bm.astype(BF16), cm.astype(BF16)
            cb = lax.dot_general(cmb, bmb, _NT, preferred_element_type=F32)
            for pi in range(pairs_per_group):
                pair = g * pairs_per_group + pi
                xs = xa[:, pair * LANES:(pair + 1) * LANES]
                h0 = 2 * pair
                dtp = jnp.where(lo, dt[:, h0:h0 + 1], dt[:, h0 + 1:h0 + 2])
                xdt = (xs * dtp).astype(BF16)
                y_diag, s_loc, a_cols, a_last = [], [], [], []
                for h in (h0, h0 + 1):
                    acol, arow = acum[:, h:h + 1], acum_t[h:h + 1, :]
                    alast = acum[CHUNK - 1:CHUNK, h:h + 1]
                    decay = jnp.exp(jnp.where(causal, acol - arow, NEG))
                    y_diag.append(jnp.dot((cb * decay).astype(BF16), xdt, preferred_element_type=F32))
                    bw = (bm * jnp.exp(alast - acol)).astype(BF16)
                    s_loc.append(lax.dot_general(bw, xdt, _TN, preferred_element_type=F32))
                    a_cols.append(acol)
                    a_last.append(alast)
                s_in = state[pair]
                y_off = jnp.dot(cmb, s_in.astype(BF16), preferred_element_type=F32)
                y = (jnp.where(lo, y_diag[0], y_diag[1])
                     + y_off * jnp.exp(jnp.where(lo, a_cols[0], a_cols[1]))
                     + xs * dsk_ref[:, pair * LANES:(pair + 1) * LANES])
                state[pair] = (s_in * jnp.exp(jnp.where(lo, a_last[0], a_last[1]))
                               + jnp.where(lo, s_loc[0], s_loc[1]))
                ys.append(y)
        y = jnp.concatenate(ys, axis=-1) * _silu(z_ref[pl.ds(t0, CHUNK), :].astype(F32))
        o_ref[pl.ds(t0, CHUNK), :] = _rms(y, nw_ref[...]).astype(o_ref.dtype)
        return carry

    lax.fori_loop(0, seq // CHUNK, chunk, 0)


def _ssd(xbc, z, gates, gates_t, conv_w, conv_b, dt_bias, a_log, d_skip, norm_w, batch, seq,
         n_heads, n_groups, d_state):
    inner = z.shape[1]
    xw = xbc.shape[1]
    pad = LANES - n_heads
    dtb = jnp.pad(dt_bias, (0, pad))
    a_neg = jnp.pad(-jnp.exp(a_log), (0, pad))
    dsk = jnp.repeat(d_skip, inner // n_heads)
    tok = lambda w: pl.BlockSpec((seq, w), lambda b: (b, 0))
    return pl.pallas_call(
        functools.partial(_ssd_kernel, seq=seq, inner=inner, n_heads=n_heads, n_groups=n_groups,
                          d_state=d_state),
        grid=(batch,),
        in_specs=[tok(xw), tok(inner), tok(LANES), pl.BlockSpec((LANES, seq), lambda b: (0, b)),
                  _resident((CONV_K, xw)), _resident((1, xw)), _resident((1, LANES)),
                  _resident((n_heads, 1)), _resident((1, LANES)), _resident((n_heads, 1)),
                  _resident((1, inner)), _resident((1, inner))],
        out_specs=tok(inner),
        out_shape=jax.ShapeDtypeStruct((batch * seq, inner), BF16),
        scratch_shapes=[pltpu.VMEM((HALO + CHUNK, xw), F32),
                        pltpu.VMEM((n_heads // 2, d_state, LANES), F32)],
        compiler_params=_cparams(("parallel",)),
        name="ssd",
    )(xbc, z, gates, gates_t, conv_w, conv_b.reshape(1, xw), dtb.reshape(1, LANES),
      dt_bias.reshape(n_heads, 1), a_neg.reshape(1, LANES), a_neg[:n_heads].reshape(n_heads, 1),
      dsk.reshape(1, inner), norm_w.reshape(1, inner))


def _mlstm_kernel(u_ref, v_ref, op_ref, g_ref, gt_ref, cw_ref, cb_ref, wq_ref, wk_ref, ib_ref, fb_ref,
                  gbt_ref, nw_ref, o_ref, win, c_st, n_st, m_st, *, seq, n_heads, i_col, i_row):
    inner = u_ref.shape[1]
    dh = inner // n_heads
    c_st[...] = jnp.zeros_like(c_st)
    n_st[...] = jnp.zeros_like(n_st)
    m_st[...] = jnp.zeros_like(m_st)
    ltri, utri = _tri(CHUNK, True), _tri(CHUNK, False)
    causal = lax.broadcasted_iota(jnp.int32, (CHUNK, CHUNK), 1) <= lax.broadcasted_iota(jnp.int32, (CHUNK, CHUNK), 0)
    is_f_row = lax.broadcasted_iota(jnp.int32, (2 * n_heads, CHUNK), 0) >= n_heads

    def chunk(c, carry):
        t0 = pl.multiple_of(c * CHUNK, CHUNK)
        uc = _conv_silu_chunk(u_ref, win, cw_ref, cb_ref, c).astype(BF16)
        q = jnp.concatenate([jnp.dot(uc[:, j * LANES:(j + 1) * LANES], wq_ref[j], preferred_element_type=F32)
                             for j in range(inner // LANES)], axis=-1)
        k = jnp.concatenate([jnp.dot(uc[:, j * LANES:(j + 1) * LANES], wk_ref[j], preferred_element_type=F32)
                             for j in range(inner // LANES)], axis=-1)
        g = g_ref[pl.ds(t0, CHUNK), :]
        ig = g + ib_ref[...]
        logf = -_softplus(-(g + fb_ref[...]))
        bcum = jnp.dot(ltri, logf, preferred_element_type=F32, precision=lax.Precision.HIGHEST)
        gt = gt_ref[i_row:i_row + 2 * n_heads, pl.ds(t0, CHUNK)] + gbt_ref[...]
        gt = jnp.where(is_f_row, -_softplus(-gt), gt)
        bcum_t = jnp.dot(gt, utri, preferred_element_type=F32, precision=lax.Precision.HIGHEST)
        vb = v_ref[pl.ds(t0, CHUNK), :]
        for h in range(n_heads):
            sl = slice(h * dh, (h + 1) * dh)
            qh, kh, vh = q[:, sl].astype(BF16), k[:, sl], vb[:, sl]
            khb = kh.astype(BF16)
            bcol = bcum[:, i_col + n_heads + h:i_col + n_heads + h + 1]
            icol = ig[:, i_col + h:i_col + h + 1]
            brow = bcum_t[n_heads + h:n_heads + h + 1, :]
            irow = gt[h:h + 1, :]
            gtot = bcol[CHUNK - 1:CHUNK, :]
            m_in = m_st[h]
            dmat = jnp.where(causal, bcol - brow + irow, NEG)
            w_end = gtot - bcol + icol
            a_loc = jnp.max(w_end, axis=0, keepdims=True)
            ke = kh * jnp.exp(w_end - a_loc)
            c_loc = lax.dot_general(ke.astype(BF16), vh, _TN, preferred_element_type=F32)
            n_loc = jnp.sum(ke, axis=0, keepdims=True)
            inter = bcol + m_in
            m_out = jnp.maximum(jnp.max(dmat, axis=-1, keepdims=True), inter)
            qk = lax.dot_general(qh, khb, _NT, preferred_element_type=F32)
            wqk = jnp.exp(dmat - m_out) * qk
            e_int = jnp.exp(inter - m_out)
            c_in, n_in = c_st[h], n_st[h]
            num = (jnp.dot(wqk.astype(BF16), vh, preferred_element_type=F32)
                   + jnp.dot(qh, c_in.astype(BF16), preferred_element_type=F32) * e_int)
            den = (jnp.sum(wqk, axis=-1, keepdims=True)
                   + jnp.sum(q[:, sl] * n_in, axis=-1, keepdims=True) * e_int)
            hd = num / jnp.maximum(jnp.abs(den), jnp.exp(-m_out))
            hd = _rms(hd, nw_ref[:, sl]) * _sigmoid(op_ref[pl.ds(t0, CHUNK), sl].astype(F32))
            o_ref[pl.ds(t0, CHUNK), sl] = hd.astype(o_ref.dtype)
            m_new = jnp.maximum(gtot + m_in, a_loc)
            d_old, d_loc = jnp.exp(gtot + m_in - m_new), jnp.exp(a_loc - m_new)
            c_st[h] = d_old * c_in + d_loc * c_loc
            n_st[h] = d_old * n_in + d_loc * n_loc
            m_st[h] = m_new
        return carry

    lax.fori_loop(0, seq // CHUNK, chunk, 0)


def _block_diag(w, scale):
    n, b, _ = w.shape
    per = LANES // b
    w = (w * scale).reshape(n // per, per, b, b)
    eye = jnp.eye(per, dtype=w.dtype)
    return jnp.einsum("gpji,pq->gpjqi", w, eye).reshape(n // per, LANES, LANES).astype(BF16)


def _mlstm(u, v, o_pre, gates, gates_t, conv_w, conv_b, wq, wk, i_bias, f_bias, norm_w, batch, seq,
           n_heads, i_col):
    inner = u.shape[1]
    dh = inner // n_heads
    ib = jnp.zeros((LANES,), F32).at[i_col:i_col + n_heads].set(i_bias)
    fb = jnp.zeros((LANES,), F32).at[i_col + n_heads:i_col + 2 * n_heads].set(f_bias)
    gbt = jnp.concatenate([i_bias, f_bias]).reshape(2 * n_heads, 1)
    tok = lambda w: pl.BlockSpec((seq, w), lambda b: (b, 0))
    return pl.pallas_call(
        functools.partial(_mlstm_kernel, seq=seq, n_heads=n_heads, i_col=i_col, i_row=i_col),
        grid=(batch,),
        in_specs=[tok(inner), tok(inner), tok(inner), tok(LANES),
                  pl.BlockSpec((LANES, seq), lambda b: (0, b)),
                  _resident((CONV_K, inner)), _resident((1, inner)),
                  _resident(wq.shape), _resident(wk.shape),
                  _resident((1, LANES)), _resident((1, LANES)), _resident((2 * n_heads, 1)),
                  _resident((1, inner))],
        out_specs=tok(inner),
        out_shape=jax.ShapeDtypeStruct((batch * seq, inner), BF16),
        scratch_shapes=[pltpu.VMEM((HALO + CHUNK, inner), F32),
                        pltpu.VMEM((n_heads, dh, dh), F32),
                        pltpu.VMEM((n_heads, 1, dh), F32),
                        pltpu.VMEM((n_heads, 1, 1), F32)],
        compiler_params=_cparams(("parallel",)),
        name="mlstm",
    )(u, v, o_pre, gates, gates_t, conv_w, conv_b.reshape(1, inner), wq, wk,
      ib.reshape(1, LANES), fb.reshape(1, LANES), gbt, norm_w.reshape(1, inner))


def kernel(x, mix_norm_w, ffn_norm_w, ab_w_in, ab_w_out, diff_lq1, diff_lk1, diff_lq2, diff_lk2, diff_subln_w, cd_w_in, c_conv_w, c_conv_b, c_dt_bias, c_a_log, c_d_skip, c_norm_w, d_conv_w, d_conv_b, d_wq, d_wk, d_i_bias, d_f_bias, d_norm_w, cd_w_out, ffn_w_gate_up, ffn_w_down, final_norm_w):
    batch, seq, d_model = x.shape
    depth = mix_norm_w.shape[0]
    a_width = d_model // 2
    a_heads = a_width // HEAD_DIM
    b_heads = d_model // (4 * HEAD_DIM)
    ffn_hidden = ffn_w_down.shape[1]
    c_heads = c_dt_bias.shape[1]
    c_state = 128
    c_groups = (c_conv_w.shape[2] - d_model) // (2 * c_state)
    d_heads = d_i_bias.shape[1]
    assert seq % (A_DILATIONS[-1] * CHUNK) == 0 and seq % B_TQ == 0
    assert a_heads % 2 == 0 and c_heads <= 16 and 2 * d_heads <= 8

    h = x.reshape(batch * seq, d_model)
    for layer in range(depth):
        j = layer // 2
        wgu = ffn_w_gate_up[layer].astype(BF16)
        wg, wu = wgu[:, :ffn_hidden], wgu[:, ffn_hidden:]
        wd = ffn_w_down[layer].astype(BF16)
        final_w = final_norm_w if layer == depth - 1 else None
        if layer % 2 == 0:
            lambda_init = 0.8 - 0.6 * math.exp(-0.3 * layer)
            qscale = jnp.ones((ab_w_in.shape[2],), F32)
            qscale = qscale.at[:a_width].set(HEAD_DIM ** -0.5)
            qscale = qscale.at[3 * a_width:3 * a_width + 2 * b_heads * HEAD_DIM].set(HEAD_DIM ** -0.5)
            w_in = (ab_w_in[j] * qscale).astype(BF16)
            (qkv,) = _norm_proj(h, mix_norm_w[layer], [w_in], [BF16])
            ya = _mixer_a(qkv, batch, seq, a_width)
            yb = _mixer_b(qkv, diff_lq1[j], diff_lk1[j], diff_lq2[j], diff_lk2[j], diff_subln_w[j],
                          batch, seq, 3 * a_width, b_heads, lambda_init)
            wo = ab_w_out[j].astype(BF16)
            h = _post(h, [ya, yb], [wo[:a_width], wo[a_width:]], ffn_norm_w[layer], wg, wu, wd, final_w)
        else:
            xbc_w = c_conv_w.shape[2]
            sizes = (d_model, xbc_w, c_heads, d_model, d_model, d_model, d_heads, d_heads)
            offs = [0]
            for s in sizes:
                offs.append(offs[-1] + s)
            w = cd_w_in[j]
            cols = lambda i: w[:, offs[i]:offs[i + 1]]
            i_col = 16
            wgate = jnp.zeros((d_model, LANES), F32)
            wgate = wgate.at[:, :c_heads].set(cols(2))
            wgate = wgate.at[:, i_col:i_col + d_heads].set(cols(6))
            wgate = wgate.at[:, i_col + d_heads:i_col + 2 * d_heads].set(cols(7)).astype(BF16)
            ws = [cols(0).astype(BF16), cols(1).astype(BF16), cols(3).astype(BF16), cols(4).astype(BF16),
                  cols(5).astype(BF16), wgate]
            z, xbc, u, v, o_pre, gates, gates_t = _norm_proj(
                h, mix_norm_w[layer], ws, [BF16] * 5 + [F32], w_t=wgate.T)
            yc = _ssd(xbc, z, gates, gates_t, c_conv_w[j], c_conv_b[j], c_dt_bias[j], c_a_log[j],
                      c_d_skip[j], c_norm_w[j], batch, seq, c_heads, c_groups, c_state)
            blk = d_wq.shape[2]
            yd = _mlstm(u, v, o_pre, gates, gates_t, d_conv_w[j], d_conv_b[j],
                        _block_diag(d_wq[j], 1.0), _block_diag(d_wk[j], (d_model // d_heads) ** -0.5),
                        d_i_bias[j], d_f_bias[j], d_norm_w[j], batch, seq, d_heads, i_col)
            del blk
            wo = cd_w_out[j].astype(BF16)
            h = _post(h, [yc, yd], [wo[:d_model], wo[d_model:]], ffn_norm_w[layer], wg, wu, wd, final_w)
    return h.reshape(batch, seq, d_model)
```

```python
import functools
import math

import jax
import jax.numpy as jnp
from jax import lax
from jax.experimental import pallas as pl
from jax.experimental.pallas import tpu as pltpu

F32 = jnp.float32
BF16 = jnp.bfloat16

RMS_EPS = 1e-6
HEAD_DIM = 64
LANES = 128
CHUNK = 128
A_DILATIONS = (1, 4, 16)
NEG = -1e30
VMEM_LIMIT = 56 * 1024 * 1024
ROW_TILE = 512
PROJ_COLS = 512
FFN_COLS = 768
B_TQ = 256

_NT = (((1,), (1,)), ((), ()))
_TN = (((0,), (0,)), ((), ()))


def _cparams(sem):
    return pltpu.CompilerParams(dimension_semantics=sem, vmem_limit_bytes=VMEM_LIMIT)


def _resident(shape):
    zeros = (0,) * len(shape)
    return pl.BlockSpec(shape, lambda *_: zeros, pipeline_mode=pl.Buffered(1))


def _rms(x, w):
    return x * lax.rsqrt(jnp.mean(x * x, axis=-1, keepdims=True) + RMS_EPS) * w


def _sigmoid(x):
    return 1.0 / (1.0 + jnp.exp(-x))


def _silu(x):
    return x * _sigmoid(x)


def _softplus(x):
    return jnp.maximum(x, 0.0) + jnp.log(1.0 + jnp.exp(-jnp.abs(x)))


def _tri(n, lower):
    r = lax.broadcasted_iota(jnp.int32, (n, n), 0)
    c = lax.broadcasted_iota(jnp.int32, (n, n), 1)
    return jnp.where((c <= r) if lower else (r <= c), 1.0, 0.0).astype(F32)


def _norm_proj_kernel(*refs, n_w, has_t):
    x_ref, nw_ref = refs[0], refs[1]
    w_refs = refs[2:2 + n_w]
    pos = 2 + n_w
    wt_ref = refs[pos] if has_t else None
    pos += int(has_t)
    o_refs = refs[pos:pos + n_w]
    ot_ref = refs[pos + n_w] if has_t else None

    xn = _rms(x_ref[...], nw_ref[...]).astype(BF16)
    for w_ref, o_ref in zip(w_refs, o_refs):
        n = w_ref.shape[1]
        for c0 in range(0, n, PROJ_COLS):
            c1 = min(n, c0 + PROJ_COLS)
            o_ref[:, c0:c1] = jnp.dot(xn, w_ref[:, c0:c1],
                                      preferred_element_type=F32).astype(o_ref.dtype)
    if has_t:
        ot_ref[...] = lax.dot_general(wt_ref[...], xn, _NT, preferred_element_type=F32)


def _norm_proj(x, norm_w, weights, out_dtypes, w_t=None):
    n_rows, d = x.shape
    tm = min(ROW_TILE, n_rows)
    has_t = w_t is not None
    in_specs = [pl.BlockSpec((tm, d), lambda i: (i, 0)), _resident((1, d))]
    in_specs += [_resident(w.shape) for w in weights]
    args = [x, norm_w.reshape(1, d)] + list(weights)
    out_shape = [jax.ShapeDtypeStruct((n_rows, w.shape[1]), dt) for w, dt in zip(weights, out_dtypes)]
    out_specs = [pl.BlockSpec((tm, w.shape[1]), lambda i: (i, 0)) for w in weights]
    if has_t:
        in_specs.append(_resident(w_t.shape))
        args.append(w_t)
        out_shape.append(jax.ShapeDtypeStruct((w_t.shape[0], n_rows), F32))
        out_specs.append(pl.BlockSpec((w_t.shape[0], tm), lambda i: (0, i)))
    return pl.pallas_call(
        functools.partial(_norm_proj_kernel, n_w=len(weights), has_t=has_t),
        grid=(n_rows // tm,),
        in_specs=in_specs, out_specs=out_specs, out_shape=out_shape,
        compiler_params=_cparams(("parallel",)),
        name="norm_proj",
    )(*args)


def _post_kernel(*refs, n_y, final):
    h_ref = refs[0]
    y_refs = refs[1:1 + n_y]
    wo_refs = refs[1 + n_y:1 + 2 * n_y]
    nw_ref, wg_ref, wu_ref, wd_ref = refs[1 + 2 * n_y:5 + 2 * n_y]
    pos = 5 + 2 * n_y
    fw_ref = refs[pos] if final else None
    pos += int(final)
    o_ref, hid_ref = refs[pos], refs[pos + 1]

    h1 = h_ref[...]
    for y_ref, wo_ref in zip(y_refs, wo_refs):
        h1 = h1 + jnp.dot(y_ref[...], wo_ref[...], preferred_element_type=F32)
    xn = _rms(h1, nw_ref[...]).astype(BF16)
    f = wg_ref.shape[1]
    for c0 in range(0, f, FFN_COLS):
        c1 = min(f, c0 + FFN_COLS)
        g = jnp.dot(xn, wg_ref[:, c0:c1], preferred_element_type=F32)
        u = jnp.dot(xn, wu_ref[:, c0:c1], preferred_element_type=F32)
        hid_ref[:, c0:c1] = (_silu(g) * u).astype(BF16)
    h2 = h1 + jnp.dot(hid_ref[...], wd_ref[...], preferred_element_type=F32)
    if final:
        h2 = _rms(h2, fw_ref[...])
    o_ref[...] = h2


def _post(h, ys, wos, ffn_norm_w, wg, wu, wd, final_w=None):
    n_rows, d = h.shape
    tm = min(ROW_TILE, n_rows)
    final = final_w is not None
    in_specs = [pl.BlockSpec((tm, d), lambda i: (i, 0))]
    in_specs += [pl.BlockSpec((tm, y.shape[1]), lambda i: (i, 0)) for y in ys]
    in_specs += [_resident(w.shape) for w in wos]
    in_specs += [_resident((1, d)), _resident(wg.shape), _resident(wu.shape), _resident(wd.shape)]
    args = [h] + list(ys) + list(wos) + [ffn_norm_w.reshape(1, d), wg, wu, wd]
    if final:
        in_specs.append(_resident((1, d)))
        args.append(final_w.reshape(1, d))
    return pl.pallas_call(
        functools.partial(_post_kernel, n_y=len(ys), final=final),
        grid=(n_rows // tm,),
        in_specs=in_specs,
        out_specs=pl.BlockSpec((tm, d), lambda i: (i, 0)),
        out_shape=jax.ShapeDtypeStruct((n_rows, d), F32),
        scratch_shapes=[pltpu.VMEM((tm, wg.shape[1]), BF16)],
        compiler_params=_cparams(("parallel",)),
        name="post_ffn",
    )(*args)


def _mixer_a_kernel(q_ref, k_ref, v_ref, o_ref, nat, f4, q4, k4, q16, k16, v1, v4, v16,
                    *stat_refs, seq):
    stats = tuple(stat_refs[3 * g:3 * g + 3] for g in range(len(A_DILATIONS)))
    d4, d16 = A_DILATIONS[1], A_DILATIONS[2]
    ratio = d16 // d4
    ts4, ts16 = seq // d4, seq // d16
    for src, dst4, dst16 in ((q_ref, q4, q16), (k_ref, k4, k16), (v_ref, v4, v16)):
        nat[...] = src[...].astype(F32)
        for r in range(d4):
            x = nat[pl.ds(r, ts4, stride=d4), :]
            f4[r * ts4:(r + 1) * ts4, :] = x
            dst4[r * ts4:(r + 1) * ts4, 0:LANES] = x.astype(BF16)
        for r in range(d16):
            x = f4[pl.ds((r % d4) * ts4 + r // d4, ts16, stride=ratio), :]
            dst16[r * ts16:(r + 1) * ts16, 0:LANES] = x.astype(BF16)
    v1[:, 0:LANES] = v_ref[...]
    for vx in (v1, v4, v16):
        vx[:, LANES:] = jnp.ones((seq, LANES), BF16)
    srcs = ((q_ref, k_ref, v1), (q4, k4, v4), (q16, k16, v16))

    lo = lax.broadcasted_iota(jnp.int32, (CHUNK, LANES), 1) < HEAD_DIM
    zero = jnp.zeros((), BF16)
    r1 = lax.broadcasted_iota(jnp.int32, (2 * CHUNK, CHUNK), 0) % CHUNK
    c1 = lax.broadcasted_iota(jnp.int32, (2 * CHUNK, CHUNK), 1)
    causal_bias = jnp.where(c1 <= r1, 0.0, NEG).astype(F32)
    r2 = lax.broadcasted_iota(jnp.int32, (2 * CHUNK, 2 * CHUNK), 0) % CHUNK
    c2 = lax.broadcasted_iota(jnp.int32, (2 * CHUNK, 2 * CHUNK), 1)
    band_bias = jnp.where((c2 >= r2) & (c2 <= r2 + CHUNK), 0.0, NEG).astype(F32)

    def attend(q, k, v, bias):
        q2 = jnp.concatenate([jnp.where(lo, q, zero), jnp.where(lo, zero, q)], axis=0)
        s = lax.dot_general(q2, k, _NT, preferred_element_type=F32) + bias
        mb = jnp.broadcast_to(jnp.max(s, axis=-1, keepdims=True), (2 * CHUNK, LANES))
        p = jnp.exp(s - jnp.concatenate([mb] * (k.shape[0] // LANES), axis=1))
        pv = jnp.dot(p.astype(BF16), v, preferred_element_type=F32)
        pick = lambda a: jnp.where(lo, a[:CHUNK], a[CHUNK:])
        return pick(mb), pick(pv[:, :LANES]), pick(pv[:, LANES:])

    def block(g, i, r, first):
        d = A_DILATIONS[g]
        qr, kr, vr = srcs[g]
        b0 = pl.multiple_of((r * (seq // (d * CHUNK)) + i) * CHUNK, CHUNK)
        qs = pl.ds(b0, CHUNK)
        if first:
            res = attend(qr[qs, :], kr[qs, :], vr[qs, :], causal_bias)
        else:
            ks = pl.ds(pl.multiple_of(b0 - CHUNK, CHUNK), 2 * CHUNK)
            res = attend(qr[qs, :], kr[ks, :], vr[ks, :], band_bias)
        start = i * (CHUNK * d) + r
        dst = pl.ds(start, CHUNK) if d == 1 else pl.ds(start, CHUNK, stride=d)
        for scr, val in zip(stats[g], res):
            scr[dst, :] = val

    unroll = 4
    for g, d in enumerate(A_DILATIONS):
        nb = seq // (d * CHUNK)
        if d <= unroll:
            for r in range(d):
                block(g, 0, r, True)
        else:
            def firsts(it, carry, g=g):
                for u in range(unroll):
                    block(g, 0, it * unroll + u, True)
                return carry
            lax.fori_loop(0, d // unroll, firsts, 0)
        if nb > 1 and d == 1:
            per = 3
            assert (nb - 1) % per == 0

            def laters1(it, carry, g=g, per=per):
                for u in range(per):
                    block(g, 1 + it * per + u, 0, False)
                return carry
            lax.fori_loop(0, (nb - 1) // per, laters1, 0)
        elif nb > 1:
            assert d <= unroll

            def laters(i, carry, g=g, d=d):
                for r in range(d):
                    block(g, i, r, False)
                return carry
            lax.fori_loop(1, nb, laters, 0)

    def merge(c, carry):
        sl = pl.ds(pl.multiple_of(c * (2 * CHUNK), 2 * CHUNK), 2 * CHUNK)
        ms = [st[0][sl, :] for st in stats]
        mx = jnp.maximum(jnp.maximum(ms[0], ms[1]), ms[2])
        ws = [jnp.exp(m - mx) for m in ms]
        num = sum(w * st[1][sl, :] for w, st in zip(ws, stats))
        den = sum(w * st[2][sl, :] for w, st in zip(ws, stats))
        o_ref[sl, :] = (num / den).astype(o_ref.dtype)
        return carry

    lax.fori_loop(0, seq // (2 * CHUNK), merge, 0)


def _mixer_a(qkv, batch, seq, a_width):
    n_pairs = a_width // LANES
    blk = lambda off: pl.BlockSpec((seq, LANES), lambda b, p: (b, off + p))
    return pl.pallas_call(
        functools.partial(_mixer_a_kernel, seq=seq),
        grid=(batch, n_pairs),
        in_specs=[blk(0), blk(n_pairs), blk(2 * n_pairs)],
        out_specs=pl.BlockSpec((seq, LANES), lambda b, p: (b, p)),
        out_shape=jax.ShapeDtypeStruct((batch * seq, a_width), BF16),
        scratch_shapes=([pltpu.VMEM((seq, LANES), F32)] * 2 + [pltpu.VMEM((seq, LANES), BF16)] * 4
                        + [pltpu.VMEM((seq, 2 * LANES), BF16)] * 3
                        + [pltpu.VMEM((seq, LANES), F32)] * (3 * len(A_DILATIONS))),
        compiler_params=_cparams(("parallel", "parallel")),
        name="mixer_a",
    )(qkv, qkv, qkv)


def _mixer_b_kernel(q_ref, k_ref, v_ref, lq1, lk1, lq2, lk2, sw_ref, o_ref, vext, *, seq, lambda_init):
    tq = B_TQ
    lam = (jnp.exp(jnp.sum(lq1[...] * lk1[...], axis=-1, keepdims=True))
           - jnp.exp(jnp.sum(lq2[...] * lk2[...], axis=-1, keepdims=True)) + lambda_init)
    vext[:, :LANES] = v_ref[...]
    vext[:, LANES:] = jnp.ones((seq, LANES), BF16)
    lo = lax.broadcasted_iota(jnp.int32, (tq, LANES), 1) < HEAD_DIM
    diag_bias = jnp.where(lax.broadcasted_iota(jnp.int32, (2 * tq, tq), 1)
                          <= lax.broadcasted_iota(jnp.int32, (2 * tq, tq), 0) % tq, 0.0, NEG).astype(F32)
    zero = jnp.zeros((), BF16)

    for i in range(seq // tq):
        q = q_ref[i * tq:(i + 1) * tq, :]
        q2 = jnp.concatenate([jnp.where(lo, q, zero), jnp.where(lo, zero, q)], axis=0)
        k0 = i * tq
        s_d = lax.dot_general(q2, k_ref[k0:k0 + tq, :], _NT, preferred_element_type=F32) + diag_bias
        m = jnp.max(s_d, axis=-1, keepdims=True)
        if i > 0:
            s_f = lax.dot_general(q2, k_ref[0:k0, :], _NT, preferred_element_type=F32)
            m = jnp.maximum(m, jnp.max(s_f, axis=-1, keepdims=True))
        pv = jnp.dot(jnp.exp(s_d - m).astype(BF16), vext[k0:k0 + tq, :], preferred_element_type=F32)
        if i > 0:
            pv = pv + jnp.dot(jnp.exp(s_f - m).astype(BF16), vext[0:k0, :], preferred_element_type=F32)
        on = pv[:, :LANES] * (1.0 / pv[:, LANES:])
        y = on[:tq] - lam * on[tq:]
        y = _rms(y, sw_ref[...]) * (1.0 - lambda_init)
        o_ref[i * tq:(i + 1) * tq, :] = y.astype(o_ref.dtype)


def _mixer_b(qkv, lq1, lk1, lq2, lk2, subln_w, batch, seq, col0, n_heads, lambda_init):
    off = col0 // LANES
    blk = lambda o: pl.BlockSpec((seq, LANES), lambda b, h: (b, o + h))
    vec = lambda n: pl.BlockSpec((1, n), lambda b, h: (0, 0))
    return pl.pallas_call(
        functools.partial(_mixer_b_kernel, seq=seq, lambda_init=lambda_init),
        grid=(batch, n_heads),
        in_specs=[blk(off), blk(off + n_heads), blk(off + 2 * n_heads),
                  vec(HEAD_DIM), vec(HEAD_DIM), vec(HEAD_DIM), vec(HEAD_DIM), vec(LANES)],
        out_specs=pl.BlockSpec((seq, LANES), lambda b, h: (b, h)),
        out_shape=jax.ShapeDtypeStruct((batch * seq, n_heads * LANES), BF16),
        scratch_shapes=[pltpu.VMEM((seq, 2 * LANES), BF16)],
        compiler_params=_cparams(("parallel", "parallel")),
        name="mixer_b",
    )(qkv, qkv, qkv, lq1.reshape(1, -1), lk1.reshape(1, -1), lq2.reshape(1, -1),
      lk2.reshape(1, -1), subln_w.reshape(1, -1))


CONV_K = 4
HALO = 16


def _conv_silu_chunk(x_ref, win, cw_ref, cb_ref, c):
    t0 = pl.multiple_of(c * CHUNK, CHUNK)
    prev = x_ref[pl.ds(pl.multiple_of(jnp.maximum(t0 - HALO, 0), HALO), HALO), :].astype(F32)
    win[0:HALO, :] = jnp.where(c > 0, prev, 0.0)
    win[HALO:HALO + CHUNK, :] = x_ref[pl.ds(t0, CHUNK), :].astype(F32)
    acc = cb_ref[...] + cw_ref[CONV_K - 1:CONV_K, :] * win[HALO:HALO + CHUNK, :]
    for j in range(CONV_K - 1):
        s = HALO - (CONV_K - 1) + j
        acc = acc + cw_ref[j:j + 1, :] * win[s:s + CHUNK, :]
    return _silu(acc)


def _ssd_kernel(xbc_ref, z_ref, g_ref, gt_ref, cw_ref, cb_ref, dtb_ref, dtbt_ref, an_ref, ant_ref,
                dsk_ref, nw_ref, o_ref, win, state, *, seq, inner, n_heads, n_groups, d_state):
    hp = inner // n_heads
    pairs_per_group = n_heads // n_groups // 2
    state[...] = jnp.zeros_like(state)
    lo = lax.broadcasted_iota(jnp.int32, (CHUNK, LANES), 1) < hp
    ltri, utri = _tri(CHUNK, True), _tri(CHUNK, False)
    causal = lax.broadcasted_iota(jnp.int32, (CHUNK, CHUNK), 1) <= lax.broadcasted_iota(jnp.int32, (CHUNK, CHUNK), 0)

    def chunk(c, carry):
        t0 = pl.multiple_of(c * CHUNK, CHUNK)
        xa = _conv_silu_chunk(xbc_ref, win, cw_ref, cb_ref, c)
        dt = _softplus(g_ref[pl.ds(t0, CHUNK), :] + dtb_ref[...])
        dt_t = _softplus(gt_ref[0:n_heads, pl.ds(t0, CHUNK)] + dtbt_ref[...])
        acum = jnp.dot(ltri, dt * an_ref[...], preferred_element_type=F32,
                       precision=lax.Precision.HIGHEST)
        acum_t = jnp.dot(dt_t * ant_ref[...], utri, preferred_element_type=F32,
                         precision=lax.Precision.HIGHEST)
        ys = []
        for g in range(n_groups):
            bm = xa[:, inner + g * d_state:inner + (g + 1) * d_state]
            cm = xa[:, inner + (n_groups + g) * d_state:inner + (n_groups + g + 1) * d_state]
            bmb, cmb = bm.astype(BF16), cm.astype(BF16)
            cb = lax.dot_general(cmb, bmb, _NT, preferred_element_type=F32)
            for pi in range(pairs_per_group):
                pair = g * pairs_per_group + pi
                xs = xa[:, pair * LANES:(pair + 1) * LANES]
                h0 = 2 * pair
                dtp = jnp.where(lo, dt[:, h0:h0 + 1], dt[:, h0 + 1:h0 + 2])
                xdt = (xs * dtp).astype(BF16)
                y_diag, s_loc, a_cols, a_last = [], [], [], []
                for h in (h0, h0 + 1):
                    acol, arow = acum[:, h:h + 1], acum_t[h:h + 1, :]
                    alast = acum[CHUNK - 1:CHUNK, h:h + 1]
                    decay = jnp.exp(jnp.where(causal, acol - arow, NEG))
                    y_diag.append(jnp.dot((cb * decay).astype(BF16), xdt, preferred_element_type=F32))
                    bw = (bm * jnp.exp(alast - acol)).astype(BF16)
                    s_loc.append(lax.dot_general(bw, xdt, _TN, preferred_element_type=F32))
                    a_cols.append(acol)
                    a_last.append(alast)
                s_in = state[pair]
                y_off = jnp.dot(cmb, s_in.astype(BF16), preferred_element_type=F32)
                y = (jnp.where(lo, y_diag[0], y_diag[1])
                     + y_off * jnp.exp(jnp.where(lo, a_cols[0], a_cols[1]))
                     + xs * dsk_ref[:, pair * LANES:(pair + 1) * LANES])
                state[pair] = (s_in * jnp.exp(jnp.where(lo, a_last[0], a_last[1]))
                               + jnp.where(lo, s_loc[0], s_loc[1]))
                ys.append(y)
        y = jnp.concatenate(ys, axis=-1) * _silu(z_ref[pl.ds(t0, CHUNK), :].astype(F32))
        o_ref[pl.ds(t0, CHUNK), :] = _rms(y, nw_ref[...]).astype(o_ref.dtype)
        return carry

    lax.fori_loop(0, seq // CHUNK, chunk, 0)


def _ssd(xbc, z, gates, gates_t, conv_w, conv_b, dt_bias, a_log, d_skip, norm_w, batch, seq,
         n_heads, n_groups, d_state):
    inner = z.shape[1]
    xw = xbc.shape[1]
    pad = LANES - n_heads
    dtb = jnp.pad(dt_bias, (0, pad))
    a_neg = jnp.pad(-jnp.exp(a_log), (0, pad))
    dsk = jnp.repeat(d_skip, inner // n_heads)
    tok = lambda w: pl.BlockSpec((seq, w), lambda b: (b, 0))
    return pl.pallas_call(
        functools.partial(_ssd_kernel, seq=seq, inner=inner, n_heads=n_heads, n_groups=n_groups,
                          d_state=d_state),
        grid=(batch,),
        in_specs=[tok(xw), tok(inner), tok(LANES), pl.BlockSpec((LANES, seq), lambda b: (0, b)),
                  _resident((CONV_K, xw)), _resident((1, xw)), _resident((1, LANES)),
                  _resident((n_heads, 1)), _resident((1, LANES)), _resident((n_heads, 1)),
                  _resident((1, inner)), _resident((1, inner))],
        out_specs=tok(inner),
        out_shape=jax.ShapeDtypeStruct((batch * seq, inner), BF16),
        scratch_shapes=[pltpu.VMEM((HALO + CHUNK, xw), F32),
                        pltpu.VMEM((n_heads // 2, d_state, LANES), F32)],
        compiler_params=_cparams(("parallel",)),
        name="ssd",
    )(xbc, z, gates, gates_t, conv_w, conv_b.reshape(1, xw), dtb.reshape(1, LANES),
      dt_bias.reshape(n_heads, 1), a_neg.reshape(1, LANES), a_neg[:n_heads].reshape(n_heads, 1),
      dsk.reshape(1, inner), norm_w.reshape(1, inner))


def _mlstm_kernel(u_ref, v_ref, op_ref, g_ref, gt_ref, cw_ref, cb_ref, wq_ref, wk_ref, ib_ref, fb_ref,
                  gbt_ref, nw_ref, o_ref, win, c_st, n_st, m_st, *, seq, n_heads, i_col, i_row):
    inner = u_ref.shape[1]
    dh = inner // n_heads
    c_st[...] = jnp.zeros_like(c_st)
    n_st[...] = jnp.zeros_like(n_st)
    m_st[...] = jnp.zeros_like(m_st)
    ltri, utri = _tri(CHUNK, True), _tri(CHUNK, False)
    causal = lax.broadcasted_iota(jnp.int32, (CHUNK, CHUNK), 1) <= lax.broadcasted_iota(jnp.int32, (CHUNK, CHUNK), 0)
    is_f_row = lax.broadcasted_iota(jnp.int32, (2 * n_heads, CHUNK), 0) >= n_heads

    def chunk(c, carry):
        t0 = pl.multiple_of(c * CHUNK, CHUNK)
        uc = _conv_silu_chunk(u_ref, win, cw_ref, cb_ref, c).astype(BF16)
        q = jnp.concatenate([jnp.dot(uc[:, j * LANES:(j + 1) * LANES], wq_ref[j], preferred_element_type=F32)
                             for j in range(inner // LANES)], axis=-1)
        k = jnp.concatenate([jnp.dot(uc[:, j * LANES:(j + 1) * LANES], wk_ref[j], preferred_element_type=F32)
                             for j in range(inner // LANES)], axis=-1)
        g = g_ref[pl.ds(t0, CHUNK), :]
        ig = g + ib_ref[...]
        logf = -_softplus(-(g + fb_ref[...]))
        bcum = jnp.dot(ltri, logf, preferred_element_type=F32, precision=lax.Precision.HIGHEST)
        gt = gt_ref[i_row:i_row + 2 * n_heads, pl.ds(t0, CHUNK)] + gbt_ref[...]
        gt = jnp.where(is_f_row, -_softplus(-gt), gt)
        bcum_t = jnp.dot(gt, utri, preferred_element_type=F32, precision=lax.Precision.HIGHEST)
        vb = v_ref[pl.ds(t0, CHUNK), :]
        for h in range(n_heads):
            sl = slice(h * dh, (h + 1) * dh)
            qh, kh, vh = q[:, sl].astype(BF16), k[:, sl], vb[:, sl]
            khb = kh.astype(BF16)
            bcol = bcum[:, i_col + n_heads + h:i_col + n_heads + h + 1]
            icol = ig[:, i_col + h:i_col + h + 1]
            brow = bcum_t[n_heads + h:n_heads + h + 1, :]
            irow = gt[h:h + 1, :]
            gtot = bcol[CHUNK - 1:CHUNK, :]
            m_in = m_st[h]
            dmat = jnp.where(causal, bcol - brow + irow, NEG)
            w_end = gtot - bcol + icol
            a_loc = jnp.max(w_end, axis=0, keepdims=True)
            ke = kh * jnp.exp(w_end - a_loc)
            c_loc = lax.dot_general(ke.astype(BF16), vh, _TN, preferred_element_type=F32)
            n_loc = jnp.sum(ke, axis=0, keepdims=True)
            inter = bcol + m_in
            m_out = jnp.maximum(jnp.max(dmat, axis=-1, keepdims=True), inter)
            qk = lax.dot_general(qh, khb, _NT, preferred_element_type=F32)
            wqk = jnp.exp(dmat - m_out) * qk
            e_int = jnp.exp(inter - m_out)
            c_in, n_in = c_st[h], n_st[h]
            num = (jnp.dot(wqk.astype(BF16), vh, preferred_element_type=F32)
                   + jnp.dot(qh, c_in.astype(BF16), preferred_element_type=F32) * e_int)
            den = (jnp.sum(wqk, axis=-1, keepdims=True)
                   + jnp.sum(q[:, sl] * n_in, axis=-1, keepdims=True) * e_int)
            hd = num / jnp.maximum(jnp.abs(den), jnp.exp(-m_out))
            hd = _rms(hd, nw_ref[:, sl]) * _sigmoid(op_ref[pl.ds(t0, CHUNK), sl].astype(F32))
            o_ref[pl.ds(t0, CHUNK), sl] = hd.astype(o_ref.dtype)
            m_new = jnp.maximum(gtot + m_in, a_loc)
            d_old, d_loc = jnp.exp(gtot + m_in - m_new), jnp.exp(a_loc - m_new)
            c_st[h] = d_old * c_in + d_loc * c_loc
            n_st[h] = d_old * n_in + d_loc * n_loc
            m_st[h] = m_new
        return carry

    lax.fori_loop(0, seq // CHUNK, chunk, 0)


def _block_diag(w, scale):
    n, b, _ = w.shape
    per = LANES // b
    w = (w * scale).reshape(n // per, per, b, b)
    eye = jnp.eye(per, dtype=w.dtype)
    return jnp.einsum("gpji,pq->gpjqi", w, eye).reshape(n // per, LANES, LANES).astype(BF16)


def _mlstm(u, v, o_pre, gates, gates_t, conv_w, conv_b, wq, wk, i_bias, f_bias, norm_w, batch, seq,
           n_heads, i_col):
    inner = u.shape[1]
    dh = inner // n_heads
    ib = jnp.zeros((LANES,), F32).at[i_col:i_col + n_heads].set(i_bias)
    fb = jnp.zeros((LANES,), F32).at[i_col + n_heads:i_col + 2 * n_heads].set(f_bias)
    gbt = jnp.concatenate([i_bias, f_bias]).reshape(2 * n_heads, 1)
    tok = lambda w: pl.BlockSpec((seq, w), lambda b: (b, 0))
    return pl.pallas_call(
        functools.partial(_mlstm_kernel, seq=seq, n_heads=n_heads, i_col=i_col, i_row=i_col),
        grid=(batch,),
        in_specs=[tok(inner), tok(inner), tok(inner), tok(LANES),
                  pl.BlockSpec((LANES, seq), lambda b: (0, b)),
                  _resident((CONV_K, inner)), _resident((1, inner)),
                  _resident(wq.shape), _resident(wk.shape),
                  _resident((1, LANES)), _resident((1, LANES)), _resident((2 * n_heads, 1)),
                  _resident((1, inner))],
        out_specs=tok(inner),
        out_shape=jax.ShapeDtypeStruct((batch * seq, inner), BF16),
        scratch_shapes=[pltpu.VMEM((HALO + CHUNK, inner), F32),
                        pltpu.VMEM((n_heads, dh, dh), F32),
                        pltpu.VMEM((n_heads, 1, dh), F32),
                        pltpu.VMEM((n_heads, 1, 1), F32)],
        compiler_params=_cparams(("parallel",)),
        name="mlstm",
    )(u, v, o_pre, gates, gates_t, conv_w, conv_b.reshape(1, inner), wq, wk,
      ib.reshape(1, LANES), fb.reshape(1, LANES), gbt, norm_w.reshape(1, inner))


def kernel(x, mix_norm_w, ffn_norm_w, ab_w_in, ab_w_out, diff_lq1, diff_lk1, diff_lq2, diff_lk2, diff_subln_w, cd_w_in, c_conv_w, c_conv_b, c_dt_bias, c_a_log, c_d_skip, c_norm_w, d_conv_w, d_conv_b, d_wq, d_wk, d_i_bias, d_f_bias, d_norm_w, cd_w_out, ffn_w_gate_up, ffn_w_down, final_norm_w):
    batch, seq, d_model = x.shape
    depth = mix_norm_w.shape[0]
    a_width = d_model // 2
    a_heads = a_width // HEAD_DIM
    b_heads = d_model // (4 * HEAD_DIM)
    ffn_hidden = ffn_w_down.shape[1]
    c_heads = c_dt_bias.shape[1]
    c_state = 128
    c_groups = (c_conv_w.shape[2] - d_model) // (2 * c_state)
    d_heads = d_i_bias.shape[1]
    assert seq % (A_DILATIONS[-1] * CHUNK) == 0 and seq % B_TQ == 0
    assert a_heads % 2 == 0 and c_heads <= 16 and 2 * d_heads <= 8

    h = x.reshape(batch * seq, d_model)
    for layer in range(depth):
        j = layer // 2
        wgu = ffn_w_gate_up[layer].astype(BF16)
        wg, wu = wgu[:, :ffn_hidden], wgu[:, ffn_hidden:]
        wd = ffn_w_down[layer].astype(BF16)
        final_w = final_norm_w if layer == depth - 1 else None
        if layer % 2 == 0:
            lambda_init = 0.8 - 0.6 * math.exp(-0.3 * layer)
            qscale = jnp.ones((ab_w_in.shape[2],), F32)
            qscale = qscale.at[:a_width].set(HEAD_DIM ** -0.5)
            qscale = qscale.at[3 * a_width:3 * a_width + 2 * b_heads * HEAD_DIM].set(HEAD_DIM ** -0.5)
            w_in = (ab_w_in[j] * qscale).astype(BF16)
            (qkv,) = _norm_proj(h, mix_norm_w[layer], [w_in], [BF16])
            ya = _mixer_a(qkv, batch, seq, a_width)
            yb = _mixer_b(qkv, diff_lq1[j], diff_lk1[j], diff_lq2[j], diff_lk2[j], diff_subln_w[j],
                          batch, seq, 3 * a_width, b_heads, lambda_init)
            wo = ab_w_out[j].astype(BF16)
            h = _post(h, [ya, yb], [wo[:a_width], wo[a_width:]], ffn_norm_w[layer], wg, wu, wd, final_w)
        else:
            xbc_w = c_conv_w.shape[2]
            sizes = (d_model, xbc_w, c_heads, d_model, d_model, d_model, d_heads, d_heads)
            offs = [0]
            for s in sizes:
                offs.append(offs[-1] + s)
            w = cd_w_in[j]
            cols = lambda i: w[:, offs[i]:offs[i + 1]]
            i_col = 16
            wgate = jnp.zeros((d_model, LANES), F32)
            wgate = wgate.at[:, :c_heads].set(cols(2))
            wgate = wgate.at[:, i_col:i_col + d_heads].set(cols(6))
            wgate = wgate.at[:, i_col + d_heads:i_col + 2 * d_heads].set(cols(7)).astype(BF16)
            ws = [cols(0).astype(BF16), cols(1).astype(BF16), cols(3).astype(BF16), cols(4).astype(BF16),
                  cols(5).astype(BF16), wgate]
            z, xbc, u, v, o_pre, gates, gates_t = _norm_proj(
                h, mix_norm_w[layer], ws, [BF16] * 5 + [F32], w_t=wgate.T)
            yc = _ssd(xbc, z, gates, gates_t, c_conv_w[j], c_conv_b[j], c_dt_bias[j], c_a_log[j],
                      c_d_skip[j], c_norm_w[j], batch, seq, c_heads, c_groups, c_state)
            blk = d_wq.shape[2]
            yd = _mlstm(u, v, o_pre, gates, gates_t, d_conv_w[j], d_conv_b[j],
                        _block_diag(d_wq[j], 1.0), _block_diag(d_wk[j], (d_model // d_heads) ** -0.5),
                        d_i_bias[j], d_f_bias[j], d_norm_w[j], batch, seq, d_heads, i_col)
            del blk
            wo = cd_w_out[j].astype(BF16)
            h = _post(h, [yc, yd], [wo[:d_model], wo[d_model:]], ffn_norm_w[layer], wg, wu, wd, final_w)
    return h.reshape(batch, seq, d_model)
```

```python
import functools
import math

import jax
import jax.numpy as jnp
from jax import lax
from jax.experimental import pallas as pl
from jax.experimental.pallas import tpu as pltpu

F32 = jnp.float32
BF16 = jnp.bfloat16

RMS_EPS = 1e-6
HEAD_DIM = 64
LANES = 128
CHUNK = 128
A_DILATIONS = (1, 4, 16)
NEG = -1e30
VMEM_LIMIT = 56 * 1024 * 1024
ROW_TILE = 512
PROJ_COLS = 512
FFN_COLS = 768
B_TQ = 256
D_UNROLL = 4

_NT = (((1,), (1,)), ((), ()))
_TN = (((0,), (0,)), ((), ()))


def _cparams(sem):
    return pltpu.CompilerParams(dimension_semantics=sem, vmem_limit_bytes=VMEM_LIMIT)


def _resident(shape):
    zeros = (0,) * len(shape)
    return pl.BlockSpec(shape, lambda *_: zeros, pipeline_mode=pl.Buffered(1))


def _rms(x, w):
    return x * lax.rsqrt(jnp.mean(x * x, axis=-1, keepdims=True) + RMS_EPS) * w


def _sigmoid(x):
    return 0.5 + 0.5 * jnp.tanh(0.5 * x)


def _silu(x):
    h = 0.5 * x
    return h + h * jnp.tanh(h)


def _softplus(x):
    return jnp.maximum(x, 0.0) + jnp.log(1.0 + jnp.exp(-jnp.abs(x)))


def _tri(n, lower):
    r = lax.broadcasted_iota(jnp.int32, (n, n), 0)
    c = lax.broadcasted_iota(jnp.int32, (n, n), 1)
    return jnp.where((c <= r) if lower else (r <= c), 1.0, 0.0).astype(F32)


def _norm_proj_kernel(*refs, n_w, has_t):
    x_ref, nw_ref = refs[0], refs[1]
    w_refs = refs[2:2 + n_w]
    pos = 2 + n_w
    wt_ref = refs[pos] if has_t else None
    pos += int(has_t)
    o_refs = refs[pos:pos + n_w]
    ot_ref = refs[pos + n_w] if has_t else None

    xn = _rms(x_ref[...], nw_ref[...]).astype(BF16)
    for w_ref, o_ref in zip(w_refs, o_refs):
        n = w_ref.shape[1]
        for c0 in range(0, n, PROJ_COLS):
            c1 = min(n, c0 + PROJ_COLS)
            o_ref[:, c0:c1] = jnp.dot(xn, w_ref[:, c0:c1],
                                      preferred_element_type=F32).astype(o_ref.dtype)
    if has_t:
        ot_ref[...] = lax.dot_general(wt_ref[...], xn, _NT, preferred_element_type=F32)


def _norm_proj(x, norm_w, weights, out_dtypes, w_t=None):
    n_rows, d = x.shape
    tm = min(ROW_TILE, n_rows)
    has_t = w_t is not None
    in_specs = [pl.BlockSpec((tm, d), lambda i: (i, 0)), _resident((1, d))]
    in_specs += [_resident(w.shape) for w in weights]
    args = [x, norm_w.reshape(1, d)] + list(weights)
    out_shape = [jax.ShapeDtypeStruct((n_rows, w.shape[1]), dt) for w, dt in zip(weights, out_dtypes)]
    out_specs = [pl.BlockSpec((tm, w.shape[1]), lambda i: (i, 0)) for w in weights]
    if has_t:
        in_specs.append(_resident(w_t.shape))
        args.append(w_t)
        out_shape.append(jax.ShapeDtypeStruct((w_t.shape[0], n_rows), F32))
        out_specs.append(pl.BlockSpec((w_t.shape[0], tm), lambda i: (0, i)))
    return pl.pallas_call(
        functools.partial(_norm_proj_kernel, n_w=len(weights), has_t=has_t),
        grid=(n_rows // tm,),
        in_specs=in_specs, out_specs=out_specs, out_shape=out_shape,
        compiler_params=_cparams(("parallel",)),
        name="norm_proj",
    )(*args)


def _post_kernel(*refs, n_y, final):
    h_ref = refs[0]
    y_refs = refs[1:1 + n_y]
    wo_refs = refs[1 + n_y:1 + 2 * n_y]
    nw_ref, wg_ref, wu_ref, wd_ref = refs[1 + 2 * n_y:5 + 2 * n_y]
    pos = 5 + 2 * n_y
    fw_ref = refs[pos] if final else None
    pos += int(final)
    o_ref, hid_ref = refs[pos], refs[pos + 1]

    h1 = h_ref[...]
    for y_ref, wo_ref in zip(y_refs, wo_refs):
        h1 = h1 + jnp.dot(y_ref[...], wo_ref[...], preferred_element_type=F32)
    xn = _rms(h1, nw_ref[...]).astype(BF16)
    f = wg_ref.shape[1]
    for c0 in range(0, f, FFN_COLS):
        c1 = min(f, c0 + FFN_COLS)
        g = jnp.dot(xn, wg_ref[:, c0:c1], preferred_element_type=F32)
        u = jnp.dot(xn, wu_ref[:, c0:c1], preferred_element_type=F32)
        hid_ref[:, c0:c1] = (_silu(g) * u).astype(BF16)
    h2 = h1 + jnp.dot(hid_ref[...], wd_ref[...], preferred_element_type=F32)
    if final:
        h2 = _rms(h2, fw_ref[...])
    o_ref[...] = h2


def _post(h, ys, wos, ffn_norm_w, wg, wu, wd, final_w=None):
    n_rows, d = h.shape
    tm = min(ROW_TILE, n_rows)
    final = final_w is not None
    in_specs = [pl.BlockSpec((tm, d), lambda i: (i, 0))]
    in_specs += [pl.BlockSpec((tm, y.shape[1]), lambda i: (i, 0)) for y in ys]
    in_specs += [_resident(w.shape) for w in wos]
    in_specs += [_resident((1, d)), _resident(wg.shape), _resident(wu.shape), _resident(wd.shape)]
    args = [h] + list(ys) + list(wos) + [ffn_norm_w.reshape(1, d), wg, wu, wd]
    if final:
        in_specs.append(_resident((1, d)))
        args.append(final_w.reshape(1, d))
    return pl.pallas_call(
        functools.partial(_post_kernel, n_y=len(ys), final=final),
        grid=(n_rows // tm,),
        in_specs=in_specs,
        out_specs=pl.BlockSpec((tm, d), lambda i: (i, 0)),
        out_shape=jax.ShapeDtypeStruct((n_rows, d), F32),
        scratch_shapes=[pltpu.VMEM((tm, wg.shape[1]), BF16)],
        compiler_params=_cparams(("parallel",)),
        name="post_ffn",
    )(*args)


def _mixer_a_kernel(q_ref, k_ref, v_ref, o_ref, nat, f4, q4, k4, q16, k16, v1, v4, v16,
                    *stat_refs, seq):
    stats = tuple(stat_refs[3 * g:3 * g + 3] for g in range(len(A_DILATIONS)))
    d4, d16 = A_DILATIONS[1], A_DILATIONS[2]
    ratio = d16 // d4
    ts4, ts16 = seq // d4, seq // d16
    for src, dst4, dst16 in ((q_ref, q4, q16), (k_ref, k4, k16), (v_ref, v4, v16)):
        nat[...] = src[...].astype(F32)
        for r in range(d4):
            x = nat[pl.ds(r, ts4, stride=d4), :]
            f4[r * ts4:(r + 1) * ts4, :] = x
            dst4[r * ts4:(r + 1) * ts4, 0:LANES] = x.astype(BF16)
        for r in range(d16):
            x = f4[pl.ds((r % d4) * ts4 + r // d4, ts16, stride=ratio), :]
            dst16[r * ts16:(r + 1) * ts16, 0:LANES] = x.astype(BF16)
    v1[:, 0:LANES] = v_ref[...]
    for vx in (v1, v4, v16):
        vx[:, LANES:] = jnp.ones((seq, LANES), BF16)
    srcs = ((q_ref, k_ref, v1), (q4, k4, v4), (q16, k16, v16))

    lo = lax.broadcasted_iota(jnp.int32, (CHUNK, LANES), 1) < HEAD_DIM
    zero = jnp.zeros((), BF16)
    r1 = lax.broadcasted_iota(jnp.int32, (2 * CHUNK, CHUNK), 0) % CHUNK
    c1 = lax.broadcasted_iota(jnp.int32, (2 * CHUNK, CHUNK), 1)
    causal_bias = jnp.where(c1 <= r1, 0.0, NEG).astype(F32)
    r2 = lax.broadcasted_iota(jnp.int32, (2 * CHUNK, 2 * CHUNK), 0) % CHUNK
    c2 = lax.broadcasted_iota(jnp.int32, (2 * CHUNK, 2 * CHUNK), 1)
    band_bias = jnp.where((c2 >= r2) & (c2 <= r2 + CHUNK), 0.0, NEG).astype(F32)

    def attend(q, k, v, bias):
        q2 = jnp.concatenate([jnp.where(lo, q, zero), jnp.where(lo, zero, q)], axis=0)
        s = lax.dot_general(q2, k, _NT, preferred_element_type=F32) + bias
        mb = jnp.broadcast_to(jnp.max(s, axis=-1, keepdims=True), (2 * CHUNK, LANES))
        p = jnp.exp(s - jnp.concatenate([mb] * (k.shape[0] // LANES), axis=1))
        pv = jnp.dot(p.astype(BF16), v, preferred_element_type=F32)
        pick = lambda a: jnp.where(lo, a[:CHUNK], a[CHUNK:])
        return pick(mb), pick(pv[:, :LANES]), pick(pv[:, LANES:])

    def block(g, i, r):
        d = A_DILATIONS[g]
        qr, kr, vr = srcs[g]
        b0 = (r * (seq // (d * CHUNK)) + i) * CHUNK
        qs = slice(b0, b0 + CHUNK)
        if i == 0:
            res = attend(qr[qs, :], kr[qs, :], vr[qs, :], causal_bias)
        else:
            ks = slice(b0 - CHUNK, b0 + CHUNK)
            res = attend(qr[qs, :], kr[ks, :], vr[ks, :], band_bias)
        start = i * (CHUNK * d) + r
        dst = pl.ds(start, CHUNK) if d == 1 else pl.ds(start, CHUNK, stride=d)
        for scr, val in zip(stats[g], res):
            scr[dst, :] = val

    for g, d in enumerate(A_DILATIONS):
        for r in range(d):
            for i in range(seq // (d * CHUNK)):
                block(g, i, r)

    def merge(c, carry):
        sl = pl.ds(pl.multiple_of(c * (2 * CHUNK), 2 * CHUNK), 2 * CHUNK)
        ms = [st[0][sl, :] for st in stats]
        mx = jnp.maximum(jnp.maximum(ms[0], ms[1]), ms[2])
        ws = [jnp.exp(m - mx) for m in ms]
        num = sum(w * st[1][sl, :] for w, st in zip(ws, stats))
        den = sum(w * st[2][sl, :] for w, st in zip(ws, stats))
        o_ref[sl, :] = (num / den).astype(o_ref.dtype)
        return carry

    lax.fori_loop(0, seq // (2 * CHUNK), merge, 0)


def _mixer_a(qkv, batch, seq, a_width):
    n_pairs = a_width // LANES
    blk = lambda off: pl.BlockSpec((seq, LANES), lambda b, p: (b, off + p))
    return pl.pallas_call(
        functools.partial(_mixer_a_kernel, seq=seq),
        grid=(batch, n_pairs),
        in_specs=[blk(0), blk(n_pairs), blk(2 * n_pairs)],
        out_specs=pl.BlockSpec((seq, LANES), lambda b, p: (b, p)),
        out_shape=jax.ShapeDtypeStruct((batch * seq, a_width), BF16),
        scratch_shapes=([pltpu.VMEM((seq, LANES), F32)] * 2 + [pltpu.VMEM((seq, LANES), BF16)] * 4
                        + [pltpu.VMEM((seq, 2 * LANES), BF16)] * 3
                        + [pltpu.VMEM((seq, LANES), F32)] * (3 * len(A_DILATIONS))),
        compiler_params=_cparams(("parallel", "parallel")),
        name="mixer_a",
    )(qkv, qkv, qkv)


def _mixer_b_kernel(q_ref, k_ref, v_ref, lq1, lk1, lq2, lk2, sw_ref, o_ref, vext, *, seq, lambda_init):
    tq = B_TQ
    lam = (jnp.exp(jnp.sum(lq1[...] * lk1[...], axis=-1, keepdims=True))
           - jnp.exp(jnp.sum(lq2[...] * lk2[...], axis=-1, keepdims=True)) + lambda_init)
    vext[:, :LANES] = v_ref[...]
    vext[:, LANES:] = jnp.ones((seq, LANES), BF16)
    lo = lax.broadcasted_iota(jnp.int32, (tq, LANES), 1) < HEAD_DIM
    diag_bias = jnp.where(lax.broadcasted_iota(jnp.int32, (2 * tq, tq), 1)
                          <= lax.broadcasted_iota(jnp.int32, (2 * tq, tq), 0) % tq, 0.0, NEG).astype(F32)
    zero = jnp.zeros((), BF16)

    for i in range(seq // tq):
        q = q_ref[i * tq:(i + 1) * tq, :]
        q2 = jnp.concatenate([jnp.where(lo, q, zero), jnp.where(lo, zero, q)], axis=0)
        k0 = i * tq
        s_d = lax.dot_general(q2, k_ref[k0:k0 + tq, :], _NT, preferred_element_type=F32) + diag_bias
        m = jnp.max(s_d, axis=-1, keepdims=True)
        if i > 0:
            s_f = lax.dot_general(q2, k_ref[0:k0, :], _NT, preferred_element_type=F32)
            m = jnp.maximum(m, jnp.max(s_f, axis=-1, keepdims=True))
        pv = jnp.dot(jnp.exp(s_d - m).astype(BF16), vext[k0:k0 + tq, :], preferred_element_type=F32)
        if i > 0:
            pv = pv + jnp.dot(jnp.exp(s_f - m).astype(BF16), vext[0:k0, :], preferred_element_type=F32)
        on = pv[:, :LANES] * (1.0 / pv[:, LANES:])
        y = on[:tq] - lam * on[tq:]
        y = _rms(y, sw_ref[...]) * (1.0 - lambda_init)
        o_ref[i * tq:(i + 1) * tq, :] = y.astype(o_ref.dtype)


def _mixer_b(qkv, lq1, lk1, lq2, lk2, subln_w, batch, seq, col0, n_heads, lambda_init):
    off = col0 // LANES
    blk = lambda o: pl.BlockSpec((seq, LANES), lambda b, h: (b, o + h))
    vec = lambda n: pl.BlockSpec((1, n), lambda b, h: (0, 0))
    return pl.pallas_call(
        functools.partial(_mixer_b_kernel, seq=seq, lambda_init=lambda_init),
        grid=(batch, n_heads),
        in_specs=[blk(off), blk(off + n_heads), blk(off + 2 * n_heads),
                  vec(HEAD_DIM), vec(HEAD_DIM), vec(HEAD_DIM), vec(HEAD_DIM), vec(LANES)],
        out_specs=pl.BlockSpec((seq, LANES), lambda b, h: (b, h)),
        out_shape=jax.ShapeDtypeStruct((batch * seq, n_heads * LANES), BF16),
        scratch_shapes=[pltpu.VMEM((seq, 2 * LANES), BF16)],
        compiler_params=_cparams(("parallel", "parallel")),
        name="mixer_b",
    )(qkv, qkv, qkv, lq1.reshape(1, -1), lk1.reshape(1, -1), lq2.reshape(1, -1),
      lk2.reshape(1, -1), subln_w.reshape(1, -1))


CONV_K = 4
HALO = 16


def _conv_silu_chunk(x_ref, win, cw_ref, cb_ref, c):
    t0 = pl.multiple_of(c * CHUNK, CHUNK)
    prev = x_ref[pl.ds(pl.multiple_of(jnp.maximum(t0 - HALO, 0), HALO), HALO), :].astype(F32)
    win[0:HALO, :] = jnp.where(c > 0, prev, 0.0)
    win[HALO:HALO + CHUNK, :] = x_ref[pl.ds(t0, CHUNK), :].astype(F32)
    acc = cb_ref[...] + cw_ref[CONV_K - 1:CONV_K, :] * win[HALO:HALO + CHUNK, :]
    for j in range(CONV_K - 1):
        s = HALO - (CONV_K - 1) + j
        acc = acc + cw_ref[j:j + 1, :] * win[s:s + CHUNK, :]
    return _silu(acc)


def _ssd_kernel(xbc_ref, z_ref, g_ref, gt_ref, cw_ref, cb_ref, dtb_ref, dtbt_ref, an_ref, ant_ref,
                dsk_ref, nw_ref, o_ref, win, state, *, seq, inner, n_heads, n_groups, d_state):
    hp = inner // n_heads
    pairs_per_group = n_heads // n_groups // 2
    state[...] = jnp.zeros_like(state)
    lo = lax.broadcasted_iota(jnp.int32, (CHUNK, LANES), 1) < hp
    ltri, utri = _tri(CHUNK, True), _tri(CHUNK, False)
    causal = lax.broadcasted_iota(jnp.int32, (CHUNK, CHUNK), 1) <= lax.broadcasted_iota(jnp.int32, (CHUNK, CHUNK), 0)

    def chunk(c, carry):
        t0 = pl.multiple_of(c * CHUNK, CHUNK)
        xa = _conv_silu_chunk(xbc_ref, win, cw_ref, cb_ref, c)
        dt = _softplus(g_ref[pl.ds(t0, CHUNK), :] + dtb_ref[...])
        dt_t = _softplus(gt_ref[0:n_heads, pl.ds(t0, CHUNK)] + dtbt_ref[...])
        acum = jnp.dot(ltri, dt * an_ref[...], preferred_element_type=F32,
                       precision=lax.Precision.HIGHEST)
        acum_t = jnp.dot(dt_t * ant_ref[...], utri, preferred_element_type=F32,
                         precision=lax.Precision.HIGHEST)
        ys = []
        for g in range(n_groups):
            bm = xa[:, inner + g * d_state:inner + (g + 1) * d_state]
            cm = xa[:, inner + (n_groups + g) * d_state:inner + (n_groups + g + 1) * d_state]
            bmb, cmb = bm.astype(BF16), cm.astype(BF16)
            cb = lax.dot_general(cmb, bmb, _NT, preferred_element_type=F32)
            for pi in range(pairs_per_group):
                pair = g * pairs_per_group + pi
                xs = xa[:, pair * LANES:(pair + 1) * LANES]
                h0 = 2 * pair
                dtp = jnp.where(lo, dt[:, h0:h0 + 1], dt[:, h0 + 1:h0 + 2])
                xdt = (xs * dtp).astype(BF16)
                y_diag, s_loc, a_cols, a_last = [], [], [], []
                for h in (h0, h0 + 1):
                    acol, arow = acum[:, h:h + 1], acum_t[h:h + 1, :]
                    alast = acum[CHUNK - 1:CHUNK, h:h + 1]
                    decay = jnp.exp(jnp.where(causal, acol - arow, NEG))
                    y_diag.append(jnp.dot((cb * decay).astype(BF16), xdt, preferred_element_type=F32))
                    bw = (bm * jnp.exp(alast - acol)).astype(BF16)
                    s_loc.append(lax.dot_general(bw, xdt, _TN, preferred_element_type=F32))
                    a_cols.append(acol)
                    a_last.append(alast)
                s_in = state[pair]
                y_off = jnp.dot(cmb, s_in.astype(BF16), preferred_element_type=F32)
                y = (jnp.where(lo, y_diag[0], y_diag[1])
                     + y_off * jnp.exp(jnp.where(lo, a_cols[0], a_cols[1]))
                     + xs * dsk_ref[:, pair * LANES:(pair + 1) * LANES])
                state[pair] = (s_in * jnp.exp(jnp.where(lo, a_last[0], a_last[1]))
                               + jnp.where(lo, s_loc[0], s_loc[1]))
                ys.append(y)
        y = jnp.concatenate(ys, axis=-1) * _silu(z_ref[pl.ds(t0, CHUNK), :].astype(F32))
        o_ref[pl.ds(t0, CHUNK), :] = _rms(y, nw_ref[...]).astype(o_ref.dtype)
        return carry

    lax.fori_loop(0, seq // (2 * CHUNK), lambda c2, carry: chunk(2 * c2 + 1, chunk(2 * c2, carry)), 0)


def _ssd(xbc, z, gates, gates_t, conv_w, conv_b, dt_bias, a_log, d_skip, norm_w, batch, seq,
         n_heads, n_groups, d_state):
    inner = z.shape[1]
    xw = xbc.shape[1]
    pad = LANES - n_heads
    dtb = jnp.pad(dt_bias, (0, pad))
    a_neg = jnp.pad(-jnp.exp(a_log), (0, pad))
    dsk = jnp.repeat(d_skip, inner // n_heads)
    tok = lambda w: pl.BlockSpec((seq, w), lambda b: (b, 0))
    return pl.pallas_call(
        functools.partial(_ssd_kernel, seq=seq, inner=inner, n_heads=n_heads, n_groups=n_groups,
                          d_state=d_state),
        grid=(batch,),
        in_specs=[tok(xw), tok(inner), tok(LANES), pl.BlockSpec((LANES, seq), lambda b: (0, b)),
                  _resident((CONV_K, xw)), _resident((1, xw)), _resident((1, LANES)),
                  _resident((n_heads, 1)), _resident((1, LANES)), _resident((n_heads, 1)),
                  _resident((1, inner)), _resident((1, inner))],
        out_specs=tok(inner),
        out_shape=jax.ShapeDtypeStruct((batch * seq, inner), BF16),
        scratch_shapes=[pltpu.VMEM((HALO + CHUNK, xw), F32),
                        pltpu.VMEM((n_heads // 2, d_state, LANES), F32)],
        compiler_params=_cparams(("parallel",)),
        name="ssd",
    )(xbc, z, gates, gates_t, conv_w, conv_b.reshape(1, xw), dtb.reshape(1, LANES),
      dt_bias.reshape(n_heads, 1), a_neg.reshape(1, LANES), a_neg[:n_heads].reshape(n_heads, 1),
      dsk.reshape(1, inner), norm_w.reshape(1, inner))


def _mlstm_kernel(u_ref, v_ref, op_ref, g_ref, gt_ref, cw_ref, cb_ref, wq_ref, wk_ref, ib_ref, fb_ref,
                  gbt_ref, nw_ref, o_ref, win, c_st, n_st, m_st, *, seq, n_heads, i_col, i_row):
    inner = u_ref.shape[1]
    dh = inner // n_heads
    c_st[...] = jnp.zeros_like(c_st)
    n_st[...] = jnp.zeros_like(n_st)
    m_st[...] = jnp.zeros_like(m_st)
    ltri, utri = _tri(CHUNK, True), _tri(CHUNK, False)
    causal = lax.broadcasted_iota(jnp.int32, (CHUNK, CHUNK), 1) <= lax.broadcasted_iota(jnp.int32, (CHUNK, CHUNK), 0)
    is_f_row = lax.broadcasted_iota(jnp.int32, (2 * n_heads, CHUNK), 0) >= n_heads

    def chunk(c, carry):
        t0 = pl.multiple_of(c * CHUNK, CHUNK)
        uc = _conv_silu_chunk(u_ref, win, cw_ref, cb_ref, c).astype(BF16)
        q = jnp.concatenate([jnp.dot(uc[:, j * LANES:(j + 1) * LANES], wq_ref[j], preferred_element_type=F32)
                             for j in range(inner // LANES)], axis=-1)
        k = jnp.concatenate([jnp.dot(uc[:, j * LANES:(j + 1) * LANES], wk_ref[j], preferred_element_type=F32)
                             for j in range(inner // LANES)], axis=-1)
        g = g_ref[pl.ds(t0, CHUNK), :]
        ig = g + ib_ref[...]
        logf = -_softplus(-(g + fb_ref[...]))
        bcum = jnp.dot(ltri, logf, preferred_element_type=F32, precision=lax.Precision.HIGHEST)
        gt = gt_ref[i_row:i_row + 2 * n_heads, pl.ds(t0, CHUNK)] + gbt_ref[...]
        gt = jnp.where(is_f_row, -_softplus(-gt), gt)
        bcum_t = jnp.dot(gt, utri, preferred_element_type=F32, precision=lax.Precision.HIGHEST)
        vb = v_ref[pl.ds(t0, CHUNK), :]
        for h in range(n_heads):
            sl = slice(h * dh, (h + 1) * dh)
            qh, kh, vh = q[:, sl].astype(BF16), k[:, sl], vb[:, sl]
            khb = kh.astype(BF16)
            bcol = bcum[:, i_col + n_heads + h:i_col + n_heads + h + 1]
            icol = ig[:, i_col + h:i_col + h + 1]
            brow = bcum_t[n_heads + h:n_heads + h + 1, :]
            irow = gt[h:h + 1, :]
            gtot = bcol[CHUNK - 1:CHUNK, :]
            m_in = m_st[h]
            dmat = jnp.where(causal, bcol - brow + irow, NEG)
            w_end = gtot - bcol + icol
            a_loc = jnp.max(w_end, axis=0, keepdims=True)
            ke = kh * jnp.exp(w_end - a_loc)
            c_loc = lax.dot_general(ke.astype(BF16), vh, _TN, preferred_element_type=F32)
            n_loc = jnp.sum(ke, axis=0, keepdims=True)
            inter = bcol + m_in
            m_out = jnp.maximum(jnp.max(dmat, axis=-1, keepdims=True), inter)
            qk = lax.dot_general(qh, khb, _NT, preferred_element_type=F32)
            wqk = jnp.exp(dmat - m_out) * qk
            e_int = jnp.exp(inter - m_out)
            c_in, n_in = c_st[h], n_st[h]
            num = (jnp.dot(wqk.astype(BF16), vh, preferred_element_type=F32)
                   + jnp.dot(qh, c_in.astype(BF16), preferred_element_type=F32) * e_int)
            den = (jnp.sum(wqk, axis=-1, keepdims=True)
                   + jnp.sum(q[:, sl] * n_in, axis=-1, keepdims=True) * e_int)
            hd = num / jnp.maximum(jnp.abs(den), jnp.exp(-m_out))
            hd = _rms(hd, nw_ref[:, sl]) * _sigmoid(op_ref[pl.ds(t0, CHUNK), sl].astype(F32))
            o_ref[pl.ds(t0, CHUNK), sl] = hd.astype(o_ref.dtype)
            m_new = jnp.maximum(gtot + m_in, a_loc)
            d_old, d_loc = jnp.exp(gtot + m_in - m_new), jnp.exp(a_loc - m_new)
            c_st[h] = d_old * c_in + d_loc * c_loc
            n_st[h] = d_old * n_in + d_loc * n_loc
            m_st[h] = m_new
        return carry

    def chunks(cg, carry):
        for u in range(D_UNROLL):
            carry = chunk(D_UNROLL * cg + u, carry)
        return carry

    lax.fori_loop(0, seq // (D_UNROLL * CHUNK), chunks, 0)


def _block_diag(w, scale):
    n, b, _ = w.shape
    per = LANES // b
    w = (w * scale).reshape(n // per, per, b, b)
    eye = jnp.eye(per, dtype=w.dtype)
    return jnp.einsum("gpji,pq->gpjqi", w, eye).reshape(n // per, LANES, LANES).astype(BF16)


def _mlstm(u, v, o_pre, gates, gates_t, conv_w, conv_b, wq, wk, i_bias, f_bias, norm_w, batch, seq,
           n_heads, i_col):
    inner = u.shape[1]
    dh = inner // n_heads
    ib = jnp.zeros((LANES,), F32).at[i_col:i_col + n_heads].set(i_bias)
    fb = jnp.zeros((LANES,), F32).at[i_col + n_heads:i_col + 2 * n_heads].set(f_bias)
    gbt = jnp.concatenate([i_bias, f_bias]).reshape(2 * n_heads, 1)
    tok = lambda w: pl.BlockSpec((seq, w), lambda b: (b, 0))
    return pl.pallas_call(
        functools.partial(_mlstm_kernel, seq=seq, n_heads=n_heads, i_col=i_col, i_row=i_col),
        grid=(batch,),
        in_specs=[tok(inner), tok(inner), tok(inner), tok(LANES),
                  pl.BlockSpec((LANES, seq), lambda b: (0, b)),
                  _resident((CONV_K, inner)), _resident((1, inner)),
                  _resident(wq.shape), _resident(wk.shape),
                  _resident((1, LANES)), _resident((1, LANES)), _resident((2 * n_heads, 1)),
                  _resident((1, inner))],
        out_specs=tok(inner),
        out_shape=jax.ShapeDtypeStruct((batch * seq, inner), BF16),
        scratch_shapes=[pltpu.VMEM((HALO + CHUNK, inner), F32),
                        pltpu.VMEM((n_heads, dh, dh), F32),
                        pltpu.VMEM((n_heads, 1, dh), F32),
                        pltpu.VMEM((n_heads, 1, 1), F32)],
        compiler_params=_cparams(("parallel",)),
        name="mlstm",
    )(u, v, o_pre, gates, gates_t, conv_w, conv_b.reshape(1, inner), wq, wk,
      ib.reshape(1, LANES), fb.reshape(1, LANES), gbt, norm_w.reshape(1, inner))


def kernel(x, mix_norm_w, ffn_norm_w, ab_w_in, ab_w_out, diff_lq1, diff_lk1, diff_lq2, diff_lk2, diff_subln_w, cd_w_in, c_conv_w, c_conv_b, c_dt_bias, c_a_log, c_d_skip, c_norm_w, d_conv_w, d_conv_b, d_wq, d_wk, d_i_bias, d_f_bias, d_norm_w, cd_w_out, ffn_w_gate_up, ffn_w_down, final_norm_w):
    batch, seq, d_model = x.shape
    depth = mix_norm_w.shape[0]
    a_width = d_model // 2
    a_heads = a_width // HEAD_DIM
    b_heads = d_model // (4 * HEAD_DIM)
    ffn_hidden = ffn_w_down.shape[1]
    c_heads = c_dt_bias.shape[1]
    c_state = 128
    c_groups = (c_conv_w.shape[2] - d_model) // (2 * c_state)
    d_heads = d_i_bias.shape[1]
    assert seq % (A_DILATIONS[-1] * CHUNK) == 0 and seq % B_TQ == 0
    assert a_heads % 2 == 0 and c_heads <= 16 and 2 * d_heads <= 8

    h = x.reshape(batch * seq, d_model)
    for layer in range(depth):
        j = layer // 2
        wgu = ffn_w_gate_up[layer].astype(BF16)
        wg, wu = wgu[:, :ffn_hidden], wgu[:, ffn_hidden:]
        wd = ffn_w_down[layer].astype(BF16)
        final_w = final_norm_w if layer == depth - 1 else None
        if layer % 2 == 0:
            lambda_init = 0.8 - 0.6 * math.exp(-0.3 * layer)
            qscale = jnp.ones((ab_w_in.shape[2],), F32)
            qscale = qscale.at[:a_width].set(HEAD_DIM ** -0.5)
            qscale = qscale.at[3 * a_width:3 * a_width + 2 * b_heads * HEAD_DIM].set(HEAD_DIM ** -0.5)
            w_in = (ab_w_in[j] * qscale).astype(BF16)
            (qkv,) = _norm_proj(h, mix_norm_w[layer], [w_in], [BF16])
            ya = _mixer_a(qkv, batch, seq, a_width)
            yb = _mixer_b(qkv, diff_lq1[j], diff_lk1[j], diff_lq2[j], diff_lk2[j], diff_subln_w[j],
                          batch, seq, 3 * a_width, b_heads, lambda_init)
            wo = ab_w_out[j].astype(BF16)
            h = _post(h, [ya, yb], [wo[:a_width], wo[a_width:]], ffn_norm_w[layer], wg, wu, wd, final_w)
        else:
            xbc_w = c_conv_w.shape[2]
            sizes = (d_model, xbc_w, c_heads, d_model, d_model, d_model, d_heads, d_heads)
            offs = [0]
            for s in sizes:
                offs.append(offs[-1] + s)
            w = cd_w_in[j]
            cols = lambda i: w[:, offs[i]:offs[i + 1]]
            i_col = 16
            wgate = jnp.zeros((d_model, LANES), F32)
            wgate = wgate.at[:, :c_heads].set(cols(2))
            wgate = wgate.at[:, i_col:i_col + d_heads].set(cols(6))
            wgate = wgate.at[:, i_col + d_heads:i_col + 2 * d_heads].set(cols(7)).astype(BF16)
            ws = [cols(0).astype(BF16), cols(1).astype(BF16), cols(3).astype(BF16), cols(4).astype(BF16),
                  cols(5).astype(BF16), wgate]
            z, xbc, u, v, o_pre, gates, gates_t = _norm_proj(
                h, mix_norm_w[layer], ws, [BF16] * 5 + [F32], w_t=wgate.T)
            yc = _ssd(xbc, z, gates, gates_t, c_conv_w[j], c_conv_b[j], c_dt_bias[j], c_a_log[j],
                      c_d_skip[j], c_norm_w[j], batch, seq, c_heads, c_groups, c_state)
            yd = _mlstm(u, v, o_pre, gates, gates_t, d_conv_w[j], d_conv_b[j],
                        _block_diag(d_wq[j], 1.0), _block_diag(d_wk[j], (d_model // d_heads) ** -0.5),
                        d_i_bias[j], d_f_bias[j], d_norm_w[j], batch, seq, d_heads, i_col)
            wo = cd_w_out[j].astype(BF16)
            h = _post(h, [yc, yd], [wo[:d_model], wo[d_model:]], ffn_norm_w[layer], wg, wu, wd, final_w)
    return h.reshape(batch, seq, d_model)
```

```python
import functools
import math

import jax
import jax.numpy as jnp
from jax import lax
from jax.experimental import pallas as pl
from jax.experimental.pallas import tpu as pltpu

F32 = jnp.float32
BF16 = jnp.bfloat16

RMS_EPS = 1e-6
HEAD_DIM = 64
LANES = 128
CHUNK = 128
A_DILATIONS = (1, 4, 16)
NEG = -1e30
VMEM_LIMIT = 56 * 1024 * 1024
ROW_TILE = 512
PROJ_COLS = 512
FFN_COLS = 768
B_TQ = 256
D_UNROLL = 4

_NT = (((1,), (1,)), ((), ()))
_TN = (((0,), (0,)), ((), ()))


def _cparams(sem):
    return pltpu.CompilerParams(dimension_semantics=sem, vmem_limit_bytes=VMEM_LIMIT)


def _resident(shape):
    zeros = (0,) * len(shape)
    return pl.BlockSpec(shape, lambda *_: zeros, pipeline_mode=pl.Buffered(1))


def _rms(x, w):
    return x * lax.rsqrt(jnp.mean(x * x, axis=-1, keepdims=True) + RMS_EPS) * w


def _sigmoid(x):
    return 0.5 + 0.5 * jnp.tanh(0.5 * x)


def _silu(x):
    h = 0.5 * x
    return h + h * jnp.tanh(h)


def _softplus(x):
    return jnp.maximum(x, 0.0) + jnp.log(1.0 + jnp.exp(-jnp.abs(x)))


def _tri(n, lower):
    r = lax.broadcasted_iota(jnp.int32, (n, n), 0)
    c = lax.broadcasted_iota(jnp.int32, (n, n), 1)
    return jnp.where((c <= r) if lower else (r <= c), 1.0, 0.0).astype(F32)


def _norm_proj_kernel(*refs, n_w, has_t):
    x_ref, nw_ref = refs[0], refs[1]
    w_refs = refs[2:2 + n_w]
    pos = 2 + n_w
    wt_ref = refs[pos] if has_t else None
    pos += int(has_t)
    o_refs = refs[pos:pos + n_w]
    ot_ref = refs[pos + n_w] if has_t else None

    xn = _rms(x_ref[...], nw_ref[...]).astype(BF16)
    for w_ref, o_ref in zip(w_refs, o_refs):
        n = w_ref.shape[1]
        for c0 in range(0, n, PROJ_COLS):
            c1 = min(n, c0 + PROJ_COLS)
            o_ref[:, c0:c1] = jnp.dot(xn, w_ref[:, c0:c1],
                                      preferred_element_type=F32).astype(o_ref.dtype)
    if has_t:
        ot_ref[...] = lax.dot_general(wt_ref[...], xn, _NT, preferred_element_type=F32)


def _norm_proj(x, norm_w, weights, out_dtypes, w_t=None):
    n_rows, d = x.shape
    tm = min(ROW_TILE, n_rows)
    has_t = w_t is not None
    in_specs = [pl.BlockSpec((tm, d), lambda i: (i, 0)), _resident((1, d))]
    in_specs += [_resident(w.shape) for w in weights]
    args = [x, norm_w.reshape(1, d)] + list(weights)
    out_shape = [jax.ShapeDtypeStruct((n_rows, w.shape[1]), dt) for w, dt in zip(weights, out_dtypes)]
    out_specs = [pl.BlockSpec((tm, w.shape[1]), lambda i: (i, 0)) for w in weights]
    if has_t:
        in_specs.append(_resident(w_t.shape))
        args.append(w_t)
        out_shape.append(jax.ShapeDtypeStruct((w_t.shape[0], n_rows), F32))
        out_specs.append(pl.BlockSpec((w_t.shape[0], tm), lambda i: (0, i)))
    return pl.pallas_call(
        functools.partial(_norm_proj_kernel, n_w=len(weights), has_t=has_t),
        grid=(n_rows // tm,),
        in_specs=in_specs, out_specs=out_specs, out_shape=out_shape,
        compiler_params=_cparams(("parallel",)),
        name="norm_proj",
    )(*args)


def _post_kernel(*refs, n_y, final):
    h_ref = refs[0]
    y_refs = refs[1:1 + n_y]
    wo_refs = refs[1 + n_y:1 + 2 * n_y]
    nw_ref, wg_ref, wu_ref, wd_ref = refs[1 + 2 * n_y:5 + 2 * n_y]
    pos = 5 + 2 * n_y
    fw_ref = refs[pos] if final else None
    pos += int(final)
    o_ref, hid_ref = refs[pos], refs[pos + 1]

    h1 = h_ref[...]
    for y_ref, wo_ref in zip(y_refs, wo_refs):
        h1 = h1 + jnp.dot(y_ref[...], wo_ref[...], preferred_element_type=F32)
    xn = _rms(h1, nw_ref[...]).astype(BF16)
    f = wg_ref.shape[1]
    for c0 in range(0, f, FFN_COLS):
        c1 = min(f, c0 + FFN_COLS)
        g = jnp.dot(xn, wg_ref[:, c0:c1], preferred_element_type=F32)
        u = jnp.dot(xn, wu_ref[:, c0:c1], preferred_element_type=F32)
        hid_ref[:, c0:c1] = (_silu(g) * u).astype(BF16)
    h2 = h1 + jnp.dot(hid_ref[...], wd_ref[...], preferred_element_type=F32)
    if final:
        h2 = _rms(h2, fw_ref[...])
    o_ref[...] = h2


def _post(h, ys, wos, ffn_norm_w, wg, wu, wd, final_w=None):
    n_rows, d = h.shape
    tm = min(ROW_TILE, n_rows)
    final = final_w is not None
    in_specs = [pl.BlockSpec((tm, d), lambda i: (i, 0))]
    in_specs += [pl.BlockSpec((tm, y.shape[1]), lambda i: (i, 0)) for y in ys]
    in_specs += [_resident(w.shape) for w in wos]
    in_specs += [_resident((1, d)), _resident(wg.shape), _resident(wu.shape), _resident(wd.shape)]
    args = [h] + list(ys) + list(wos) + [ffn_norm_w.reshape(1, d), wg, wu, wd]
    if final:
        in_specs.append(_resident((1, d)))
        args.append(final_w.reshape(1, d))
    return pl.pallas_call(
        functools.partial(_post_kernel, n_y=len(ys), final=final),
        grid=(n_rows // tm,),
        in_specs=in_specs,
        out_specs=pl.BlockSpec((tm, d), lambda i: (i, 0)),
        out_shape=jax.ShapeDtypeStruct((n_rows, d), F32),
        scratch_shapes=[pltpu.VMEM((tm, wg.shape[1]), BF16)],
        compiler_params=_cparams(("parallel",)),
        name="post_ffn",
    )(*args)


def _mixer_a_kernel(q_ref, k_ref, v_ref, o_ref, nat, f4, q4, k4, q16, k16, v1, v4, v16,
                    *stat_refs, seq):
    stats = tuple(stat_refs[3 * g:3 * g + 3] for g in range(len(A_DILATIONS)))
    d4, d16 = A_DILATIONS[1], A_DILATIONS[2]
    ratio = d16 // d4
    ts4, ts16 = seq // d4, seq // d16
    for src, dst4, dst16 in ((q_ref, q4, q16), (k_ref, k4, k16), (v_ref, v4, v16)):
        nat[...] = src[...].astype(F32)
        for r in range(d4):
            x = nat[pl.ds(r, ts4, stride=d4), :]
            f4[r * ts4:(r + 1) * ts4, :] = x
            dst4[r * ts4:(r + 1) * ts4, 0:LANES] = x.astype(BF16)
        for r in range(d16):
            x = f4[pl.ds((r % d4) * ts4 + r // d4, ts16, stride=ratio), :]
            dst16[r * ts16:(r + 1) * ts16, 0:LANES] = x.astype(BF16)
    v1[:, 0:LANES] = v_ref[...]
    for vx in (v1, v4, v16):
        vx[:, LANES:] = jnp.ones((seq, LANES), BF16)
    srcs = ((q_ref, k_ref, v1), (q4, k4, v4), (q16, k16, v16))

    lo = lax.broadcasted_iota(jnp.int32, (CHUNK, LANES), 1) < HEAD_DIM
    zero = jnp.zeros((), BF16)
    r1 = lax.broadcasted_iota(jnp.int32, (2 * CHUNK, CHUNK), 0) % CHUNK
    c1 = lax.broadcasted_iota(jnp.int32, (2 * CHUNK, CHUNK), 1)
    causal_bias = jnp.where(c1 <= r1, 0.0, NEG).astype(F32)
    r2 = lax.broadcasted_iota(jnp.int32, (2 * CHUNK, 2 * CHUNK), 0) % CHUNK
    c2 = lax.broadcasted_iota(jnp.int32, (2 * CHUNK, 2 * CHUNK), 1)
    band_bias = jnp.where((c2 >= r2) & (c2 <= r2 + CHUNK), 0.0, NEG).astype(F32)

    def attend(q, k, v, bias):
        q2 = jnp.concatenate([jnp.where(lo, q, zero), jnp.where(lo, zero, q)], axis=0)
        s = lax.dot_general(q2, k, _NT, preferred_element_type=F32) + bias
        mb = jnp.broadcast_to(jnp.max(s, axis=-1, keepdims=True), (2 * CHUNK, LANES))
        p = jnp.exp(s - jnp.concatenate([mb] * (k.shape[0] // LANES), axis=1))
        pv = jnp.dot(p.astype(BF16), v, preferred_element_type=F32)
        pick = lambda a: jnp.where(lo, a[:CHUNK], a[CHUNK:])
        return pick(mb), pick(pv[:, :LANES]), pick(pv[:, LANES:])

    def block(g, i, r):
        d = A_DILATIONS[g]
        qr, kr, vr = srcs[g]
        b0 = (r * (seq // (d * CHUNK)) + i) * CHUNK
        qs = slice(b0, b0 + CHUNK)
        if i == 0:
            res = attend(qr[qs, :], kr[qs, :], vr[qs, :], causal_bias)
        else:
            ks = slice(b0 - CHUNK, b0 + CHUNK)
            res = attend(qr[qs, :], kr[ks, :], vr[ks, :], band_bias)
        if d == d16:
            dst = pl.ds((r % d4) * ts4 + r // d4, CHUNK, stride=ratio)
        else:
            dst = qs
        for scr, val in zip(stats[g], res):
            scr[dst, :] = val

    for g, d in enumerate(A_DILATIONS):
        for r in range(d):
            for i in range(seq // (d * CHUNK)):
                block(g, i, r)

    step = 2 * CHUNK
    for c in range(seq // step):
        r, m0 = (c * step) // ts4, (c * step) % ts4
        sl = slice(c * step, (c + 1) * step)
        nat_rows = pl.ds(d4 * m0 + r, step, stride=d4)
        views = (nat_rows, sl, sl)
        ms = [st[0][v, :] for st, v in zip(stats, views)]
        mx = jnp.maximum(jnp.maximum(ms[0], ms[1]), ms[2])
        ws = [jnp.exp(m - mx) for m in ms]
        num = sum(w * st[1][v, :] for w, st, v in zip(ws, stats, views))
        den = sum(w * st[2][v, :] for w, st, v in zip(ws, stats, views))
        nat[nat_rows, :] = num / den
    o_ref[...] = nat[...].astype(o_ref.dtype)


def _mixer_a(qkv, batch, seq, a_width):
    n_pairs = a_width // LANES
    blk = lambda off: pl.BlockSpec((seq, LANES), lambda b, p: (b, off + p))
    return pl.pallas_call(
        functools.partial(_mixer_a_kernel, seq=seq),
        grid=(batch, n_pairs),
        in_specs=[blk(0), blk(n_pairs), blk(2 * n_pairs)],
        out_specs=pl.BlockSpec((seq, LANES), lambda b, p: (b, p)),
        out_shape=jax.ShapeDtypeStruct((batch * seq, a_width), BF16),
        scratch_shapes=([pltpu.VMEM((seq, LANES), F32)] * 2 + [pltpu.VMEM((seq, LANES), BF16)] * 4
                        + [pltpu.VMEM((seq, 2 * LANES), BF16)] * 3
                        + [pltpu.VMEM((seq, LANES), F32)] * (3 * len(A_DILATIONS))),
        compiler_params=_cparams(("parallel", "parallel")),
        name="mixer_a",
    )(qkv, qkv, qkv)


def _mixer_b_kernel(q_ref, k_ref, v_ref, lq1, lk1, lq2, lk2, sw_ref, o_ref, vext, *, seq, lambda_init):
    tq = B_TQ
    lam = (jnp.exp(jnp.sum(lq1[...] * lk1[...], axis=-1, keepdims=True))
           - jnp.exp(jnp.sum(lq2[...] * lk2[...], axis=-1, keepdims=True)) + lambda_init)
    vext[:, :LANES] = v_ref[...]
    vext[:, LANES:] = jnp.ones((seq, LANES), BF16)
    lo = lax.broadcasted_iota(jnp.int32, (tq, LANES), 1) < HEAD_DIM
    diag_bias = jnp.where(lax.broadcasted_iota(jnp.int32, (2 * tq, tq), 1)
                          <= lax.broadcasted_iota(jnp.int32, (2 * tq, tq), 0) % tq, 0.0, NEG).astype(F32)
    zero = jnp.zeros((), BF16)

    for i in range(seq // tq):
        q = q_ref[i * tq:(i + 1) * tq, :]
        q2 = jnp.concatenate([jnp.where(lo, q, zero), jnp.where(lo, zero, q)], axis=0)
        k0 = i * tq
        s_d = lax.dot_general(q2, k_ref[k0:k0 + tq, :], _NT, preferred_element_type=F32) + diag_bias
        m = jnp.max(s_d, axis=-1, keepdims=True)
        if i > 0:
            s_f = lax.dot_general(q2, k_ref[0:k0, :], _NT, preferred_element_type=F32)
            m = jnp.maximum(m, jnp.max(s_f, axis=-1, keepdims=True))
        pv = jnp.dot(jnp.exp(s_d - m).astype(BF16), vext[k0:k0 + tq, :], preferred_element_type=F32)
        if i > 0:
            pv = pv + jnp.dot(jnp.exp(s_f - m).astype(BF16), vext[0:k0, :], preferred_element_type=F32)
        on = pv[:, :LANES] * (1.0 / pv[:, LANES:])
        y = on[:tq] - lam * on[tq:]
        y = _rms(y, sw_ref[...]) * (1.0 - lambda_init)
        o_ref[i * tq:(i + 1) * tq, :] = y.astype(o_ref.dtype)


def _mixer_b(qkv, lq1, lk1, lq2, lk2, subln_w, batch, seq, col0, n_heads, lambda_init):
    off = col0 // LANES
    blk = lambda o: pl.BlockSpec((seq, LANES), lambda b, h: (b, o + h))
    vec = lambda n: pl.BlockSpec((1, n), lambda b, h: (0, 0))
    return pl.pallas_call(
        functools.partial(_mixer_b_kernel, seq=seq, lambda_init=lambda_init),
        grid=(batch, n_heads),
        in_specs=[blk(off), blk(off + n_heads), blk(off + 2 * n_heads),
                  vec(HEAD_DIM), vec(HEAD_DIM), vec(HEAD_DIM), vec(HEAD_DIM), vec(LANES)],
        out_specs=pl.BlockSpec((seq, LANES), lambda b, h: (b, h)),
        out_shape=jax.ShapeDtypeStruct((batch * seq, n_heads * LANES), BF16),
        scratch_shapes=[pltpu.VMEM((seq, 2 * LANES), BF16)],
        compiler_params=_cparams(("parallel", "parallel")),
        name="mixer_b",
    )(qkv, qkv, qkv, lq1.reshape(1, -1), lk1.reshape(1, -1), lq2.reshape(1, -1),
      lk2.reshape(1, -1), subln_w.reshape(1, -1))


CONV_K = 4
HALO = 16


def _conv_silu_chunk(x_ref, cw_ref, cb_ref, c):
    t0 = pl.multiple_of(c * CHUNK, CHUNK)
    prev = x_ref[pl.ds(pl.multiple_of(jnp.maximum(t0 - HALO, 0), HALO), HALO), :].astype(F32)
    cur = x_ref[pl.ds(t0, CHUNK), :].astype(F32)
    xw = jnp.concatenate([jnp.where(c > 0, prev, 0.0), cur], axis=0)
    acc = cb_ref[...] + cw_ref[CONV_K - 1:CONV_K, :] * cur
    for j in range(CONV_K - 1):
        shifted = pltpu.roll(xw, CONV_K - 1 - j, axis=0)[HALO:, :]
        acc = acc + cw_ref[j:j + 1, :] * shifted
    return _silu(acc)


def _ssd_kernel(xbc_ref, z_ref, g_ref, gt_ref, cw_ref, cb_ref, dtb_ref, dtbt_ref, an_ref, ant_ref,
                dsk_ref, nw_ref, o_ref, state, *, seq, inner, n_heads, n_groups, d_state):
    hp = inner // n_heads
    pairs_per_group = n_heads // n_groups // 2
    state[...] = jnp.zeros_like(state)
    lo = lax.broadcasted_iota(jnp.int32, (CHUNK, LANES), 1) < hp
    ltri, utri = _tri(CHUNK, True), _tri(CHUNK, False)
    causal = lax.broadcasted_iota(jnp.int32, (CHUNK, CHUNK), 1) <= lax.broadcasted_iota(jnp.int32, (CHUNK, CHUNK), 0)

    def chunk(c, carry):
        t0 = pl.multiple_of(c * CHUNK, CHUNK)
        xa = _conv_silu_chunk(xbc_ref, cw_ref, cb_ref, c)
        dt = _softplus(g_ref[pl.ds(t0, CHUNK), :] + dtb_ref[...])
        dt_t = _softplus(gt_ref[0:n_heads, pl.ds(t0, CHUNK)] + dtbt_ref[...])
        acum = jnp.dot(ltri, dt * an_ref[...], preferred_element_type=F32,
                       precision=lax.Precision.HIGHEST)
        acum_t = jnp.dot(dt_t * ant_ref[...], utri, preferred_element_type=F32,
                         precision=lax.Precision.HIGHEST)
        ys = []
        for g in range(n_groups):
            bm = xa[:, inner + g * d_state:inner + (g + 1) * d_state]
            cm = xa[:, inner + (n_groups + g) * d_state:inner + (n_groups + g + 1) * d_state]
            bmb, cmb = bm.astype(BF16), cm.astype(BF16)
            cb = lax.dot_general(cmb, bmb, _NT, preferred_element_type=F32)
            for pi in range(pairs_per_group):
                pair = g * pairs_per_group + pi
                xs = xa[:, pair * LANES:(pair + 1) * LANES]
                h0 = 2 * pair
                dtp = jnp.where(lo, dt[:, h0:h0 + 1], dt[:, h0 + 1:h0 + 2])
                xdt = (xs * dtp).astype(BF16)
                y_diag, s_loc, a_cols, a_last = [], [], [], []
                for h in (h0, h0 + 1):
                    acol, arow = acum[:, h:h + 1], acum_t[h:h + 1, :]
                    alast = acum[CHUNK - 1:CHUNK, h:h + 1]
                    decay = jnp.exp(jnp.where(causal, acol - arow, NEG))
                    y_diag.append(jnp.dot((cb * decay).astype(BF16), xdt, preferred_element_type=F32))
                    bw = (bm * jnp.exp(alast - acol)).astype(BF16)
                    s_loc.append(lax.dot_general(bw, xdt, _TN, preferred_element_type=F32))
                    a_cols.append(acol)
                    a_last.append(alast)
                s_in = state[pair]
                y_off = jnp.dot(cmb, s_in.astype(BF16), preferred_element_type=F32)
                y = (jnp.where(lo, y_diag[0], y_diag[1])
                     + y_off * jnp.exp(jnp.where(lo, a_cols[0], a_cols[1]))
                     + xs * dsk_ref[:, pair * LANES:(pair + 1) * LANES])
                state[pair] = (s_in * jnp.exp(jnp.where(lo, a_last[0], a_last[1]))
                               + jnp.where(lo, s_loc[0], s_loc[1]))
                ys.append(y)
        y = jnp.concatenate(ys, axis=-1) * _silu(z_ref[pl.ds(t0, CHUNK), :].astype(F32))
        o_ref[pl.ds(t0, CHUNK), :] = _rms(y, nw_ref[...]).astype(o_ref.dtype)
        return carry

    lax.fori_loop(0, seq // (2 * CHUNK), lambda c2, carry: chunk(2 * c2 + 1, chunk(2 * c2, carry)), 0)


def _ssd(xbc, z, gates, gates_t, conv_w, conv_b, dt_bias, a_log, d_skip, norm_w, batch, seq,
         n_heads, n_groups, d_state):
    inner = z.shape[1]
    xw = xbc.shape[1]
    pad = LANES - n_heads
    dtb = jnp.pad(dt_bias, (0, pad))
    a_neg = jnp.pad(-jnp.exp(a_log), (0, pad))
    dsk = jnp.repeat(d_skip, inner // n_heads)
    tok = lambda w: pl.BlockSpec((seq, w), lambda b: (b, 0))
    return pl.pallas_call(
        functools.partial(_ssd_kernel, seq=seq, inner=inner, n_heads=n_heads, n_groups=n_groups,
                          d_state=d_state),
        grid=(batch,),
        in_specs=[tok(xw), tok(inner), tok(LANES), pl.BlockSpec((LANES, seq), lambda b: (0, b)),
                  _resident((CONV_K, xw)), _resident((1, xw)), _resident((1, LANES)),
                  _resident((n_heads, 1)), _resident((1, LANES)), _resident((n_heads, 1)),
                  _resident((1, inner)), _resident((1, inner))],
        out_specs=tok(inner),
        out_shape=jax.ShapeDtypeStruct((batch * seq, inner), BF16),
        scratch_shapes=[pltpu.VMEM((n_heads // 2, d_state, LANES), F32)],
        compiler_params=_cparams(("parallel",)),
        name="ssd",
    )(xbc, z, gates, gates_t, conv_w, conv_b.reshape(1, xw), dtb.reshape(1, LANES),
      dt_bias.reshape(n_heads, 1), a_neg.reshape(1, LANES), a_neg[:n_heads].reshape(n_heads, 1),
      dsk.reshape(1, inner), norm_w.reshape(1, inner))


def _mlstm_kernel(u_ref, v_ref, op_ref, g_ref, gt_ref, cw_ref, cb_ref, wq_ref, wk_ref, ib_ref, fb_ref,
                  gbt_ref, nw_ref, o_ref, c_st, n_st, m_st, *, seq, n_heads, i_col, i_row):
    inner = u_ref.shape[1]
    dh = inner // n_heads
    c_st[...] = jnp.zeros_like(c_st)
    n_st[...] = jnp.zeros_like(n_st)
    m_st[...] = jnp.zeros_like(m_st)
    ltri, utri = _tri(CHUNK, True), _tri(CHUNK, False)
    causal = lax.broadcasted_iota(jnp.int32, (CHUNK, CHUNK), 1) <= lax.broadcasted_iota(jnp.int32, (CHUNK, CHUNK), 0)
    is_f_row = lax.broadcasted_iota(jnp.int32, (2 * n_heads, CHUNK), 0) >= n_heads

    def chunk(c, carry):
        t0 = pl.multiple_of(c * CHUNK, CHUNK)
        uc = _conv_silu_chunk(u_ref, cw_ref, cb_ref, c).astype(BF16)
        q = jnp.concatenate([jnp.dot(uc[:, j * LANES:(j + 1) * LANES], wq_ref[j], preferred_element_type=F32)
                             for j in range(inner // LANES)], axis=-1)
        k = jnp.concatenate([jnp.dot(uc[:, j * LANES:(j + 1) * LANES], wk_ref[j], preferred_element_type=F32)
                             for j in range(inner // LANES)], axis=-1)
        g = g_ref[pl.ds(t0, CHUNK), :]
        ig = g + ib_ref[...]
        logf = -_softplus(-(g + fb_ref[...]))
        bcum = jnp.dot(ltri, logf, preferred_element_type=F32, precision=lax.Precision.HIGHEST)
        gt = gt_ref[i_row:i_row + 2 * n_heads, pl.ds(t0, CHUNK)] + gbt_ref[...]
        gt = jnp.where(is_f_row, -_softplus(-gt), gt)
        bcum_t = jnp.dot(gt, utri, preferred_element_type=F32, precision=lax.Precision.HIGHEST)
        vb = v_ref[pl.ds(t0, CHUNK), :]
        for h in range(n_heads):
            sl = slice(h * dh, (h + 1) * dh)
            qh, kh, vh = q[:, sl].astype(BF16), k[:, sl], vb[:, sl]
            khb = kh.astype(BF16)
            bcol = bcum[:, i_col + n_heads + h:i_col + n_heads + h + 1]
            icol = ig[:, i_col + h:i_col + h + 1]
            brow = bcum_t[n_heads + h:n_heads + h + 1, :]
            irow = gt[h:h + 1, :]
            gtot = bcol[CHUNK - 1:CHUNK, :]
            m_in = m_st[h]
            dmat = jnp.where(causal, bcol - brow + irow, NEG)
            w_end = gtot - bcol + icol
            a_loc = jnp.max(w_end, axis=0, keepdims=True)
            ke = kh * jnp.exp(w_end - a_loc)
            c_loc = lax.dot_general(ke.astype(BF16), vh, _TN, preferred_element_type=F32)
            n_loc = jnp.sum(ke, axis=0, keepdims=True)
            inter = bcol + m_in
            m_out = jnp.maximum(jnp.max(dmat, axis=-1, keepdims=True), inter)
            qk = lax.dot_general(qh, khb, _NT, preferred_element_type=F32)
            wqk = jnp.exp(dmat - m_out) * qk
            e_int = jnp.exp(inter - m_out)
            c_in, n_in = c_st[h], n_st[h]
            num = (jnp.dot(wqk.astype(BF16), vh, preferred_element_type=F32)
                   + jnp.dot(qh, c_in.astype(BF16), preferred_element_type=F32) * e_int)
            den = (jnp.sum(wqk, axis=-1, keepdims=True)
                   + jnp.sum(q[:, sl] * n_in, axis=-1, keepdims=True) * e_int)
            hd = num / jnp.maximum(jnp.abs(den), jnp.exp(-m_out))
            hd = _rms(hd, nw_ref[:, sl]) * _sigmoid(op_ref[pl.ds(t0, CHUNK), sl].astype(F32))
            o_ref[pl.ds(t0, CHUNK), sl] = hd.astype(o_ref.dtype)
            m_new = jnp.maximum(gtot + m_in, a_loc)
            d_old, d_loc = jnp.exp(gtot + m_in - m_new), jnp.exp(a_loc - m_new)
            c_st[h] = d_old * c_in + d_loc * c_loc
            n_st[h] = d_old * n_in + d_loc * n_loc
            m_st[h] = m_new
        return carry

    def chunks(cg, carry):
        for u in range(D_UNROLL):
            carry = chunk(D_UNROLL * cg + u, carry)
        return carry

    lax.fori_loop(0, seq // (D_UNROLL * CHUNK), chunks, 0)


def _block_diag(w, scale):
    n, b, _ = w.shape
    per = LANES // b
    w = (w * scale).reshape(n // per, per, b, b)
    eye = jnp.eye(per, dtype=w.dtype)
    return jnp.einsum("gpji,pq->gpjqi", w, eye).reshape(n // per, LANES, LANES).astype(BF16)


def _mlstm(u, v, o_pre, gates, gates_t, conv_w, conv_b, wq, wk, i_bias, f_bias, norm_w, batch, seq,
           n_heads, i_col):
    inner = u.shape[1]
    dh = inner // n_heads
    ib = jnp.zeros((LANES,), F32).at[i_col:i_col + n_heads].set(i_bias)
    fb = jnp.zeros((LANES,), F32).at[i_col + n_heads:i_col + 2 * n_heads].set(f_bias)
    gbt = jnp.concatenate([i_bias, f_bias]).reshape(2 * n_heads, 1)
    tok = lambda w: pl.BlockSpec((seq, w), lambda b: (b, 0))
    return pl.pallas_call(
        functools.partial(_mlstm_kernel, seq=seq, n_heads=n_heads, i_col=i_col, i_row=i_col),
        grid=(batch,),
        in_specs=[tok(inner), tok(inner), tok(inner), tok(LANES),
                  pl.BlockSpec((LANES, seq), lambda b: (0, b)),
                  _resident((CONV_K, inner)), _resident((1, inner)),
                  _resident(wq.shape), _resident(wk.shape),
                  _resident((1, LANES)), _resident((1, LANES)), _resident((2 * n_heads, 1)),
                  _resident((1, inner))],
        out_specs=tok(inner),
        out_shape=jax.ShapeDtypeStruct((batch * seq, inner), BF16),
        scratch_shapes=[pltpu.VMEM((n_heads, dh, dh), F32),
                        pltpu.VMEM((n_heads, 1, dh), F32),
                        pltpu.VMEM((n_heads, 1, 1), F32)],
        compiler_params=_cparams(("parallel",)),
        name="mlstm",
    )(u, v, o_pre, gates, gates_t, conv_w, conv_b.reshape(1, inner), wq, wk,
      ib.reshape(1, LANES), fb.reshape(1, LANES), gbt, norm_w.reshape(1, inner))


def kernel(x, mix_norm_w, ffn_norm_w, ab_w_in, ab_w_out, diff_lq1, diff_lk1, diff_lq2, diff_lk2, diff_subln_w, cd_w_in, c_conv_w, c_conv_b, c_dt_bias, c_a_log, c_d_skip, c_norm_w, d_conv_w, d_conv_b, d_wq, d_wk, d_i_bias, d_f_bias, d_norm_w, cd_w_out, ffn_w_gate_up, ffn_w_down, final_norm_w):
    batch, seq, d_model = x.shape
    depth = mix_norm_w.shape[0]
    a_width = d_model // 2
    a_heads = a_width // HEAD_DIM
    b_heads = d_model // (4 * HEAD_DIM)
    ffn_hidden = ffn_w_down.shape[1]
    c_heads = c_dt_bias.shape[1]
    c_state = 128
    c_groups = (c_conv_w.shape[2] - d_model) // (2 * c_state)
    d_heads = d_i_bias.shape[1]
    assert seq % (A_DILATIONS[-1] * CHUNK) == 0 and seq % B_TQ == 0
    assert a_heads % 2 == 0 and c_heads <= 16 and 2 * d_heads <= 8

    h = x.reshape(batch * seq, d_model)
    for layer in range(depth):
        j = layer // 2
        wgu = ffn_w_gate_up[layer].astype(BF16)
        wg, wu = wgu[:, :ffn_hidden], wgu[:, ffn_hidden:]
        wd = ffn_w_down[layer].astype(BF16)
        final_w = final_norm_w if layer == depth - 1 else None
        if layer % 2 == 0:
            lambda_init = 0.8 - 0.6 * math.exp(-0.3 * layer)
            qscale = jnp.ones((ab_w_in.shape[2],), F32)
            qscale = qscale.at[:a_width].set(HEAD_DIM ** -0.5)
            qscale = qscale.at[3 * a_width:3 * a_width + 2 * b_heads * HEAD_DIM].set(HEAD_DIM ** -0.5)
            w_in = (ab_w_in[j] * qscale).astype(BF16)
            (qkv,) = _norm_proj(h, mix_norm_w[layer], [w_in], [BF16])
            ya = _mixer_a(qkv, batch, seq, a_width)
            yb = _mixer_b(qkv, diff_lq1[j], diff_lk1[j], diff_lq2[j], diff_lk2[j], diff_subln_w[j],
                          batch, seq, 3 * a_width, b_heads, lambda_init)
            wo = ab_w_out[j].astype(BF16)
            h = _post(h, [ya, yb], [wo[:a_width], wo[a_width:]], ffn_norm_w[layer], wg, wu, wd, final_w)
        else:
            xbc_w = c_conv_w.shape[2]
            sizes = (d_model, xbc_w, c_heads, d_model, d_model, d_model, d_heads, d_heads)
            offs = [0]
            for s in sizes:
                offs.append(offs[-1] + s)
            w = cd_w_in[j]
            cols = lambda i: w[:, offs[i]:offs[i + 1]]
            i_col = 16
            wgate = jnp.zeros((d_model, LANES), F32)
            wgate = wgate.at[:, :c_heads].set(cols(2))
            wgate = wgate.at[:, i_col:i_col + d_heads].set(cols(6))
            wgate = wgate.at[:, i_col + d_heads:i_col + 2 * d_heads].set(cols(7)).astype(BF16)
            ws = [cols(0).astype(BF16), cols(1).astype(BF16), cols(3).astype(BF16), cols(4).astype(BF16),
                  cols(5).astype(BF16), wgate]
            z, xbc, u, v, o_pre, gates, gates_t = _norm_proj(
                h, mix_norm_w[layer], ws, [BF16] * 5 + [F32], w_t=wgate.T)
            yc = _ssd(xbc, z, gates, gates_t, c_conv_w[j], c_conv_b[j], c_dt_bias[j], c_a_log[j],
                      c_d_skip[j], c_norm_w[j], batch, seq, c_heads, c_groups, c_state)
            yd = _mlstm(u, v, o_pre, gates, gates_t, d_conv_w[j], d_conv_b[j],
                        _block_diag(d_wq[j], 1.0), _block_diag(d_wk[j], (d_model // d_heads) ** -0.5),
                        d_i_bias[j], d_f_bias[j], d_norm_w[j], batch, seq, d_heads, i_col)
            wo = cd_w_out[j].astype(BF16)
            h = _post(h, [yc, yd], [wo[:d_model], wo[d_model:]], ffn_norm_w[layer], wg, wu, wd, final_w)
    return h.reshape(batch, seq, d_model)
```

```python
import functools
import math

import jax
import jax.numpy as jnp
from jax import lax
from jax.experimental import pallas as pl
from jax.experimental.pallas import tpu as pltpu

F32 = jnp.float32
BF16 = jnp.bfloat16

RMS_EPS = 1e-6
HEAD_DIM = 64
LANES = 128
CHUNK = 128
A_DILATIONS = (1, 4, 16)
NEG = -1e30
VMEM_LIMIT = 56 * 1024 * 1024
ROW_TILE = 512
PROJ_COLS = 512
FFN_COLS = 768
B_TQ = 256
C_UNROLL = 2
D_UNROLL = 4

_NT = (((1,), (1,)), ((), ()))
_TN = (((0,), (0,)), ((), ()))


def _cparams(sem):
    return pltpu.CompilerParams(dimension_semantics=sem, vmem_limit_bytes=VMEM_LIMIT)


def _resident(shape):
    zeros = (0,) * len(shape)
    return pl.BlockSpec(shape, lambda *_: zeros, pipeline_mode=pl.Buffered(1))


def _rms(x, w):
    return x * lax.rsqrt(jnp.mean(x * x, axis=-1, keepdims=True) + RMS_EPS) * w


def _sigmoid(x):
    return 0.5 + 0.5 * jnp.tanh(0.5 * x)


def _silu(x):
    h = 0.5 * x
    return h + h * jnp.tanh(h)


def _softplus(x):
    return jnp.maximum(x, 0.0) + jnp.log(1.0 + jnp.exp(-jnp.abs(x)))


def _tri(n, lower):
    r = lax.broadcasted_iota(jnp.int32, (n, n), 0)
    c = lax.broadcasted_iota(jnp.int32, (n, n), 1)
    return jnp.where((c <= r) if lower else (r <= c), 1.0, 0.0).astype(F32)


def _norm_proj_kernel(*refs, n_w, has_t):
    x_ref, nw_ref = refs[0], refs[1]
    w_refs = refs[2:2 + n_w]
    pos = 2 + n_w
    wt_ref = refs[pos] if has_t else None
    pos += int(has_t)
    o_refs = refs[pos:pos + n_w]
    ot_ref = refs[pos + n_w] if has_t else None

    xn = _rms(x_ref[...], nw_ref[...]).astype(BF16)
    for w_ref, o_ref in zip(w_refs, o_refs):
        n = w_ref.shape[1]
        for c0 in range(0, n, PROJ_COLS):
            c1 = min(n, c0 + PROJ_COLS)
            o_ref[:, c0:c1] = jnp.dot(xn, w_ref[:, c0:c1],
                                      preferred_element_type=F32).astype(o_ref.dtype)
    if has_t:
        ot_ref[...] = lax.dot_general(wt_ref[...], xn, _NT, preferred_element_type=F32)


def _norm_proj(x, norm_w, weights, out_dtypes, w_t=None):
    n_rows, d = x.shape
    tm = min(ROW_TILE, n_rows)
    has_t = w_t is not None
    in_specs = [pl.BlockSpec((tm, d), lambda i: (i, 0)), _resident((1, d))]
    in_specs += [_resident(w.shape) for w in weights]
    args = [x, norm_w.reshape(1, d)] + list(weights)
    out_shape = [jax.ShapeDtypeStruct((n_rows, w.shape[1]), dt) for w, dt in zip(weights, out_dtypes)]
    out_specs = [pl.BlockSpec((tm, w.shape[1]), lambda i: (i, 0)) for w in weights]
    if has_t:
        in_specs.append(_resident(w_t.shape))
        args.append(w_t)
        out_shape.append(jax.ShapeDtypeStruct((w_t.shape[0], n_rows), F32))
        out_specs.append(pl.BlockSpec((w_t.shape[0], tm), lambda i: (0, i)))
    return pl.pallas_call(
        functools.partial(_norm_proj_kernel, n_w=len(weights), has_t=has_t),
        grid=(n_rows // tm,),
        in_specs=in_specs, out_specs=out_specs, out_shape=out_shape,
        compiler_params=_cparams(("parallel",)),
        name="norm_proj",
    )(*args)


def _post_kernel(*refs, n_y, final):
    h_ref = refs[0]
    y_refs = refs[1:1 + n_y]
    wo_refs = refs[1 + n_y:1 + 2 * n_y]
    nw_ref, wg_ref, wu_ref, wd_ref = refs[1 + 2 * n_y:5 + 2 * n_y]
    pos = 5 + 2 * n_y
    fw_ref = refs[pos] if final else None
    pos += int(final)
    o_ref, hid_ref = refs[pos], refs[pos + 1]

    h1 = h_ref[...]
    for y_ref, wo_ref in zip(y_refs, wo_refs):
        h1 = h1 + jnp.dot(y_ref[...], wo_ref[...], preferred_element_type=F32)
    xn = _rms(h1, nw_ref[...]).astype(BF16)
    f = wg_ref.shape[1]
    for c0 in range(0, f, FFN_COLS):
        c1 = min(f, c0 + FFN_COLS)
        g = jnp.dot(xn, wg_ref[:, c0:c1], preferred_element_type=F32)
        u = jnp.dot(xn, wu_ref[:, c0:c1], preferred_element_type=F32)
        hid_ref[:, c0:c1] = (_silu(g) * u).astype(BF16)
    h2 = h1 + jnp.dot(hid_ref[...], wd_ref[...], preferred_element_type=F32)
    if final:
        h2 = _rms(h2, fw_ref[...])
    o_ref[...] = h2


def _post(h, ys, wos, ffn_norm_w, wg, wu, wd, final_w=None):
    n_rows, d = h.shape
    tm = min(ROW_TILE, n_rows)
    final = final_w is not None
    in_specs = [pl.BlockSpec((tm, d), lambda i: (i, 0))]
    in_specs += [pl.BlockSpec((tm, y.shape[1]), lambda i: (i, 0)) for y in ys]
    in_specs += [_resident(w.shape) for w in wos]
    in_specs += [_resident((1, d)), _resident(wg.shape), _resident(wu.shape), _resident(wd.shape)]
    args = [h] + list(ys) + list(wos) + [ffn_norm_w.reshape(1, d), wg, wu, wd]
    if final:
        in_specs.append(_resident((1, d)))
        args.append(final_w.reshape(1, d))
    return pl.pallas_call(
        functools.partial(_post_kernel, n_y=len(ys), final=final),
        grid=(n_rows // tm,),
        in_specs=in_specs,
        out_specs=pl.BlockSpec((tm, d), lambda i: (i, 0)),
        out_shape=jax.ShapeDtypeStruct((n_rows, d), F32),
        scratch_shapes=[pltpu.VMEM((tm, wg.shape[1]), BF16)],
        compiler_params=_cparams(("parallel",)),
        name="post_ffn",
    )(*args)


def _mixer_a_kernel(q_ref, k_ref, v_ref, o_ref, nat, f4, q4, k4, q16, k16, v1, v4, v16,
                    *stat_refs, seq):
    stats = tuple(stat_refs[3 * g:3 * g + 3] for g in range(len(A_DILATIONS)))
    d4, d16 = A_DILATIONS[1], A_DILATIONS[2]
    ratio = d16 // d4
    ts4, ts16 = seq // d4, seq // d16
    for src, dst4, dst16 in ((q_ref, q4, q16), (k_ref, k4, k16), (v_ref, v4, v16)):
        nat[...] = src[...].astype(F32)
        for r in range(d4):
            x = nat[pl.ds(r, ts4, stride=d4), :]
            f4[r * ts4:(r + 1) * ts4, :] = x
            dst4[r * ts4:(r + 1) * ts4, 0:LANES] = x.astype(BF16)
        for r in range(d16):
            x = f4[pl.ds((r % d4) * ts4 + r // d4, ts16, stride=ratio), :]
            dst16[r * ts16:(r + 1) * ts16, 0:LANES] = x.astype(BF16)
    v1[:, 0:LANES] = v_ref[...]
    for vx in (v1, v4, v16):
        vx[:, LANES:] = jnp.ones((seq, LANES), BF16)
    srcs = ((q_ref, k_ref, v1), (q4, k4, v4), (q16, k16, v16))

    lo = lax.broadcasted_iota(jnp.int32, (CHUNK, LANES), 1) < HEAD_DIM
    zero = jnp.zeros((), BF16)
    r1 = lax.broadcasted_iota(jnp.int32, (2 * CHUNK, CHUNK), 0) % CHUNK
    c1 = lax.broadcasted_iota(jnp.int32, (2 * CHUNK, CHUNK), 1)
    causal_bias = jnp.where(c1 <= r1, 0.0, NEG).astype(F32)
    r2 = lax.broadcasted_iota(jnp.int32, (2 * CHUNK, 2 * CHUNK), 0) % CHUNK
    c2 = lax.broadcasted_iota(jnp.int32, (2 * CHUNK, 2 * CHUNK), 1)
    band_bias = jnp.where((c2 >= r2) & (c2 <= r2 + CHUNK), 0.0, NEG).astype(F32)

    def attend(q, k, v, bias):
        q2 = jnp.concatenate([jnp.where(lo, q, zero), jnp.where(lo, zero, q)], axis=0)
        s = lax.dot_general(q2, k, _NT, preferred_element_type=F32) + bias
        mb = jnp.broadcast_to(jnp.max(s, axis=-1, keepdims=True), (2 * CHUNK, LANES))
        p = jnp.exp(s - jnp.concatenate([mb] * (k.shape[0] // LANES), axis=1))
        pv = jnp.dot(p.astype(BF16), v, preferred_element_type=F32)
        pick = lambda a: jnp.where(lo, a[:CHUNK], a[CHUNK:])
        return pick(mb), pick(pv[:, :LANES]), pick(pv[:, LANES:])

    def block(g, i, r):
        d = A_DILATIONS[g]
        qr, kr, vr = srcs[g]
        b0 = (r * (seq // (d * CHUNK)) + i) * CHUNK
        qs = slice(b0, b0 + CHUNK)
        if i == 0:
            res = attend(qr[qs, :], kr[qs, :], vr[qs, :], causal_bias)
        else:
            ks = slice(b0 - CHUNK, b0 + CHUNK)
            res = attend(qr[qs, :], kr[ks, :], vr[ks, :], band_bias)
        if d == d16:
            dst = pl.ds((r % d4) * ts4 + r // d4, CHUNK, stride=ratio)
        else:
            dst = qs
        for scr, val in zip(stats[g], res):
            scr[dst, :] = val

    for g, d in enumerate(A_DILATIONS):
        for r in range(d):
            for i in range(seq // (d * CHUNK)):
                block(g, i, r)

    step = 2 * CHUNK
    for c in range(seq // step):
        r, m0 = (c * step) // ts4, (c * step) % ts4
        sl = slice(c * step, (c + 1) * step)
        nat_rows = pl.ds(d4 * m0 + r, step, stride=d4)
        views = (nat_rows, sl, sl)
        ms = [st[0][v, :] for st, v in zip(stats, views)]
        mx = jnp.maximum(jnp.maximum(ms[0], ms[1]), ms[2])
        ws = [jnp.exp(m - mx) for m in ms]
        num = sum(w * st[1][v, :] for w, st, v in zip(ws, stats, views))
        den = sum(w * st[2][v, :] for w, st, v in zip(ws, stats, views))
        nat[nat_rows, :] = num / den
    o_ref[...] = nat[...].astype(o_ref.dtype)


def _mixer_a(qkv, batch, seq, a_width):
    n_pairs = a_width // LANES
    blk = lambda off: pl.BlockSpec((seq, LANES), lambda b, p: (b, off + p))
    return pl.pallas_call(
        functools.partial(_mixer_a_kernel, seq=seq),
        grid=(batch, n_pairs),
        in_specs=[blk(0), blk(n_pairs), blk(2 * n_pairs)],
        out_specs=pl.BlockSpec((seq, LANES), lambda b, p: (b, p)),
        out_shape=jax.ShapeDtypeStruct((batch * seq, a_width), BF16),
        scratch_shapes=([pltpu.VMEM((seq, LANES), F32)] * 2 + [pltpu.VMEM((seq, LANES), BF16)] * 4
                        + [pltpu.VMEM((seq, 2 * LANES), BF16)] * 3
                        + [pltpu.VMEM((seq, LANES), F32)] * (3 * len(A_DILATIONS))),
        compiler_params=_cparams(("parallel", "parallel")),
        name="mixer_a",
    )(qkv, qkv, qkv)


def _mixer_b_kernel(q_ref, k_ref, v_ref, lq1, lk1, lq2, lk2, sw_ref, o_ref, vext, *, seq, lambda_init):
    tq = B_TQ
    lam = (jnp.exp(jnp.sum(lq1[...] * lk1[...], axis=-1, keepdims=True))
           - jnp.exp(jnp.sum(lq2[...] * lk2[...], axis=-1, keepdims=True)) + lambda_init)
    vext[:, :LANES] = v_ref[...]
    vext[:, LANES:] = jnp.ones((seq, LANES), BF16)
    lo = lax.broadcasted_iota(jnp.int32, (tq, LANES), 1) < HEAD_DIM
    diag_bias = jnp.where(lax.broadcasted_iota(jnp.int32, (2 * tq, tq), 1)
                          <= lax.broadcasted_iota(jnp.int32, (2 * tq, tq), 0) % tq, 0.0, NEG).astype(F32)
    zero = jnp.zeros((), BF16)

    for i in range(seq // tq):
        q = q_ref[i * tq:(i + 1) * tq, :]
        q2 = jnp.concatenate([jnp.where(lo, q, zero), jnp.where(lo, zero, q)], axis=0)
        k0 = i * tq
        s_d = lax.dot_general(q2, k_ref[k0:k0 + tq, :], _NT, preferred_element_type=F32) + diag_bias
        m = jnp.max(s_d, axis=-1, keepdims=True)
        if i > 0:
            s_f = lax.dot_general(q2, k_ref[0:k0, :], _NT, preferred_element_type=F32)
            m = jnp.maximum(m, jnp.max(s_f, axis=-1, keepdims=True))
        pv = jnp.dot(jnp.exp(s_d - m).astype(BF16), vext[k0:k0 + tq, :], preferred_element_type=F32)
        if i > 0:
            pv = pv + jnp.dot(jnp.exp(s_f - m).astype(BF16), vext[0:k0, :], preferred_element_type=F32)
        on = pv[:, :LANES] * (1.0 / pv[:, LANES:])
        y = on[:tq] - lam * on[tq:]
        y = _rms(y, sw_ref[...]) * (1.0 - lambda_init)
        o_ref[i * tq:(i + 1) * tq, :] = y.astype(o_ref.dtype)


def _mixer_b(qkv, lq1, lk1, lq2, lk2, subln_w, batch, seq, col0, n_heads, lambda_init):
    off = col0 // LANES
    blk = lambda o: pl.BlockSpec((seq, LANES), lambda b, h: (b, o + h))
    vec = lambda n: pl.BlockSpec((1, n), lambda b, h: (0, 0))
    return pl.pallas_call(
        functools.partial(_mixer_b_kernel, seq=seq, lambda_init=lambda_init),
        grid=(batch, n_heads),
        in_specs=[blk(off), blk(off + n_heads), blk(off + 2 * n_heads),
                  vec(HEAD_DIM), vec(HEAD_DIM), vec(HEAD_DIM), vec(HEAD_DIM), vec(LANES)],
        out_specs=pl.BlockSpec((seq, LANES), lambda b, h: (b, h)),
        out_shape=jax.ShapeDtypeStruct((batch * seq, n_heads * LANES), BF16),
        scratch_shapes=[pltpu.VMEM((seq, 2 * LANES), BF16)],
        compiler_params=_cparams(("parallel", "parallel")),
        name="mixer_b",
    )(qkv, qkv, qkv, lq1.reshape(1, -1), lk1.reshape(1, -1), lq2.reshape(1, -1),
      lk2.reshape(1, -1), subln_w.reshape(1, -1))


CONV_K = 4
HALO = 16


def _conv_silu_chunk(x_ref, cw_ref, cb_ref, c):
    t0 = pl.multiple_of(c * CHUNK, CHUNK)
    prev = x_ref[pl.ds(pl.multiple_of(jnp.maximum(t0 - HALO, 0), HALO), HALO), :].astype(F32)
    cur = x_ref[pl.ds(t0, CHUNK), :].astype(F32)
    xw = jnp.concatenate([jnp.where(c > 0, prev, 0.0), cur], axis=0)
    acc = cb_ref[...] + cw_ref[CONV_K - 1:CONV_K, :] * cur
    for j in range(CONV_K - 1):
        shifted = pltpu.roll(xw, CONV_K - 1 - j, axis=0)[HALO:, :]
        acc = acc + cw_ref[j:j + 1, :] * shifted
    return _silu(acc)


def _ssd_kernel(xbc_ref, z_ref, g_ref, gt_ref, cw_ref, cb_ref, dtb_ref, dtbt_ref, an_ref, ant_ref,
                dsk_ref, nw_ref, o_ref, state, *, seq, inner, n_heads, n_groups, d_state):
    hp = inner // n_heads
    pairs_per_group = n_heads // n_groups // 2
    state[...] = jnp.zeros_like(state)
    lo = lax.broadcasted_iota(jnp.int32, (CHUNK, LANES), 1) < hp
    zero = jnp.zeros((), BF16)
    ltri, utri = _tri(CHUNK, True), _tri(CHUNK, False)
    causal = lax.broadcasted_iota(jnp.int32, (CHUNK, CHUNK), 1) <= lax.broadcasted_iota(jnp.int32, (CHUNK, CHUNK), 0)

    def chunk(c, carry):
        t0 = pl.multiple_of(c * CHUNK, CHUNK)
        xa = _conv_silu_chunk(xbc_ref, cw_ref, cb_ref, c)
        dt = _softplus(g_ref[pl.ds(t0, CHUNK), :] + dtb_ref[...])
        dt_t = _softplus(gt_ref[0:n_heads, pl.ds(t0, CHUNK)] + dtbt_ref[...])
        acum = jnp.dot(ltri, dt * an_ref[...], preferred_element_type=F32,
                       precision=lax.Precision.HIGHEST)
        acum_t = jnp.dot(dt_t * ant_ref[...], utri, preferred_element_type=F32,
                         precision=lax.Precision.HIGHEST)
        ys = []
        for g in range(n_groups):
            bm = xa[:, inner + g * d_state:inner + (g + 1) * d_state]
            cm = xa[:, inner + (n_groups + g) * d_state:inner + (n_groups + g + 1) * d_state]
            bmb, cmb = bm.astype(BF16), cm.astype(BF16)
            cb = lax.dot_general(cmb, bmb, _NT, preferred_element_type=F32)
            for pi in range(pairs_per_group):
                pair = g * pairs_per_group + pi
                xs = xa[:, pair * LANES:(pair + 1) * LANES]
                h0 = 2 * pair
                dtp = jnp.where(lo, dt[:, h0:h0 + 1], dt[:, h0 + 1:h0 + 2])
                xdt = (xs * dtp).astype(BF16)
                xdt_h = (jnp.where(lo, xdt, zero), jnp.where(lo, zero, xdt))
                y_diag, s_loc, a_cols, a_last = 0.0, 0.0, [], []
                for h, xh in zip((h0, h0 + 1), xdt_h):
                    acol, arow = acum[:, h:h + 1], acum_t[h:h + 1, :]
                    alast = acum[CHUNK - 1:CHUNK, h:h + 1]
                    decay = jnp.exp(jnp.where(causal, acol - arow, NEG))
                    y_diag = y_diag + jnp.dot((cb * decay).astype(BF16), xh, preferred_element_type=F32)
                    bw = (bm * jnp.exp(alast - acol)).astype(BF16)
                    s_loc = s_loc + lax.dot_general(bw, xh, _TN, preferred_element_type=F32)
                    a_cols.append(acol)
                    a_last.append(alast)
                s_in = state[pair]
                y_off = jnp.dot(cmb, s_in.astype(BF16), preferred_element_type=F32)
                y = (y_diag + y_off * jnp.exp(jnp.where(lo, a_cols[0], a_cols[1]))
                     + xs * dsk_ref[:, pair * LANES:(pair + 1) * LANES])
                state[pair] = s_in * jnp.exp(jnp.where(lo, a_last[0], a_last[1])) + s_loc
                ys.append(y)
        y = jnp.concatenate(ys, axis=-1) * _silu(z_ref[pl.ds(t0, CHUNK), :].astype(F32))
        o_ref[pl.ds(t0, CHUNK), :] = _rms(y, nw_ref[...]).astype(o_ref.dtype)
        return carry

    def chunks(cg, carry):
        for u in range(C_UNROLL):
            carry = chunk(C_UNROLL * cg + u, carry)
        return carry

    lax.fori_loop(0, seq // (C_UNROLL * CHUNK), chunks, 0)


def _ssd(xbc, z, gates, gates_t, conv_w, conv_b, dt_bias, a_log, d_skip, norm_w, batch, seq,
         n_heads, n_groups, d_state):
    inner = z.shape[1]
    xw = xbc.shape[1]
    pad = LANES - n_heads
    dtb = jnp.pad(dt_bias, (0, pad))
    a_neg = jnp.pad(-jnp.exp(a_log), (0, pad))
    dsk = jnp.repeat(d_skip, inner // n_heads)
    tok = lambda w: pl.BlockSpec((seq, w), lambda b: (b, 0))
    return pl.pallas_call(
        functools.partial(_ssd_kernel, seq=seq, inner=inner, n_heads=n_heads, n_groups=n_groups,
                          d_state=d_state),
        grid=(batch,),
        in_specs=[tok(xw), tok(inner), tok(LANES), pl.BlockSpec((LANES, seq), lambda b: (0, b)),
                  _resident((CONV_K, xw)), _resident((1, xw)), _resident((1, LANES)),
                  _resident((n_heads, 1)), _resident((1, LANES)), _resident((n_heads, 1)),
                  _resident((1, inner)), _resident((1, inner))],
        out_specs=tok(inner),
        out_shape=jax.ShapeDtypeStruct((batch * seq, inner), BF16),
        scratch_shapes=[pltpu.VMEM((n_heads // 2, d_state, LANES), F32)],
        compiler_params=_cparams(("parallel",)),
        name="ssd",
    )(xbc, z, gates, gates_t, conv_w, conv_b.reshape(1, xw), dtb.reshape(1, LANES),
      dt_bias.reshape(n_heads, 1), a_neg.reshape(1, LANES), a_neg[:n_heads].reshape(n_heads, 1),
      dsk.reshape(1, inner), norm_w.reshape(1, inner))


def _mlstm_kernel(u_ref, v_ref, op_ref, g_ref, gt_ref, cw_ref, cb_ref, wq_ref, wk_ref, ib_ref, fb_ref,
                  gbt_ref, nw_ref, o_ref, c_st, n_st, m_st, *, seq, n_heads, i_col, i_row):
    inner = u_ref.shape[1]
    dh = inner // n_heads
    c_st[...] = jnp.zeros_like(c_st)
    n_st[...] = jnp.zeros_like(n_st)
    m_st[...] = jnp.zeros_like(m_st)
    ltri, utri = _tri(CHUNK, True), _tri(CHUNK, False)
    causal = lax.broadcasted_iota(jnp.int32, (CHUNK, CHUNK), 1) <= lax.broadcasted_iota(jnp.int32, (CHUNK, CHUNK), 0)
    is_f_row = lax.broadcasted_iota(jnp.int32, (2 * n_heads, CHUNK), 0) >= n_heads

    def chunk(c, carry):
        t0 = pl.multiple_of(c * CHUNK, CHUNK)
        uc = _conv_silu_chunk(u_ref, cw_ref, cb_ref, c).astype(BF16)
        q = jnp.concatenate([jnp.dot(uc[:, j * LANES:(j + 1) * LANES], wq_ref[j], preferred_element_type=F32)
                             for j in range(inner // LANES)], axis=-1)
        k = jnp.concatenate([jnp.dot(uc[:, j * LANES:(j + 1) * LANES], wk_ref[j], preferred_element_type=F32)
                             for j in range(inner // LANES)], axis=-1)
        g = g_ref[pl.ds(t0, CHUNK), :]
        ig = g + ib_ref[...]
        logf = -_softplus(-(g + fb_ref[...]))
        bcum = jnp.dot(ltri, logf, preferred_element_type=F32, precision=lax.Precision.HIGHEST)
        gt = gt_ref[i_row:i_row + 2 * n_heads, pl.ds(t0, CHUNK)] + gbt_ref[...]
        gt = jnp.where(is_f_row, -_softplus(-gt), gt)
        bcum_t = jnp.dot(gt, utri, preferred_element_type=F32, precision=lax.Precision.HIGHEST)
        vb = v_ref[pl.ds(t0, CHUNK), :]
        for h in range(n_heads):
            sl = slice(h * dh, (h + 1) * dh)
            qh, kh, vh = q[:, sl].astype(BF16), k[:, sl], vb[:, sl]
            khb = kh.astype(BF16)
            bcol = bcum[:, i_col + n_heads + h:i_col + n_heads + h + 1]
            icol = ig[:, i_col + h:i_col + h + 1]
            brow = bcum_t[n_heads + h:n_heads + h + 1, :]
            irow = gt[h:h + 1, :]
            gtot = bcol[CHUNK - 1:CHUNK, :]
            m_in = m_st[h]
            dmat = jnp.where(causal, bcol - brow + irow, NEG)
            w_end = gtot - bcol + icol
            a_loc = jnp.max(w_end, axis=0, keepdims=True)
            m_new = jnp.maximum(gtot + m_in, a_loc)
            ke = kh * jnp.exp(w_end - m_new)
            c_loc = lax.dot_general(ke.astype(BF16), vh, _TN, preferred_element_type=F32)
            n_loc = jnp.sum(ke, axis=0, keepdims=True)
            inter = bcol + m_in
            m_out = jnp.maximum(jnp.max(dmat, axis=-1, keepdims=True), inter)
            qk = lax.dot_general(qh, khb, _NT, preferred_element_type=F32)
            wqk = jnp.exp(dmat - m_out) * qk
            e_int = jnp.exp(inter - m_out)
            c_in, n_in = c_st[h], n_st[h]
            num = (jnp.dot(wqk.astype(BF16), vh, preferred_element_type=F32)
                   + jnp.dot(qh, c_in.astype(BF16), preferred_element_type=F32) * e_int)
            den = (jnp.sum(wqk, axis=-1, keepdims=True)
                   + jnp.sum(q[:, sl] * n_in, axis=-1, keepdims=True) * e_int)
            hd = num / jnp.maximum(jnp.abs(den), jnp.exp(-m_out))
            hd = _rms(hd, nw_ref[:, sl]) * _sigmoid(op_ref[pl.ds(t0, CHUNK), sl].astype(F32))
            o_ref[pl.ds(t0, CHUNK), sl] = hd.astype(o_ref.dtype)
            d_old = jnp.exp(gtot + m_in - m_new)
            c_st[h] = d_old * c_in + c_loc
            n_st[h] = d_old * n_in + n_loc
            m_st[h] = m_new
        return carry

    def chunks(cg, carry):
        for u in range(D_UNROLL):
            carry = chunk(D_UNROLL * cg + u, carry)
        return carry

    lax.fori_loop(0, seq // (D_UNROLL * CHUNK), chunks, 0)


def _block_diag(w, scale):
    n, b, _ = w.shape
    per = LANES // b
    w = (w * scale).reshape(n // per, per, b, b)
    eye = jnp.eye(per, dtype=w.dtype)
    return jnp.einsum("gpji,pq->gpjqi", w, eye).reshape(n // per, LANES, LANES).astype(BF16)


def _mlstm(u, v, o_pre, gates, gates_t, conv_w, conv_b, wq, wk, i_bias, f_bias, norm_w, batch, seq,
           n_heads, i_col):
    inner = u.shape[1]
    dh = inner // n_heads
    ib = jnp.zeros((LANES,), F32).at[i_col:i_col + n_heads].set(i_bias)
    fb = jnp.zeros((LANES,), F32).at[i_col + n_heads:i_col + 2 * n_heads].set(f_bias)
    gbt = jnp.concatenate([i_bias, f_bias]).reshape(2 * n_heads, 1)
    tok = lambda w: pl.BlockSpec((seq, w), lambda b: (b, 0))
    return pl.pallas_call(
        functools.partial(_mlstm_kernel, seq=seq, n_heads=n_heads, i_col=i_col, i_row=i_col),
        grid=(batch,),
        in_specs=[tok(inner), tok(inner), tok(inner), tok(LANES),
                  pl.BlockSpec((LANES, seq), lambda b: (0, b)),
                  _resident((CONV_K, inner)), _resident((1, inner)),
                  _resident(wq.shape), _resident(wk.shape),
                  _resident((1, LANES)), _resident((1, LANES)), _resident((2 * n_heads, 1)),
                  _resident((1, inner))],
        out_specs=tok(inner),
        out_shape=jax.ShapeDtypeStruct((batch * seq, inner), BF16),
        scratch_shapes=[pltpu.VMEM((n_heads, dh, dh), F32),
                        pltpu.VMEM((n_heads, 1, dh), F32),
                        pltpu.VMEM((n_heads, 1, 1), F32)],
        compiler_params=_cparams(("parallel",)),
        name="mlstm",
    )(u, v, o_pre, gates, gates_t, conv_w, conv_b.reshape(1, inner), wq, wk,
      ib.reshape(1, LANES), fb.reshape(1, LANES), gbt, norm_w.reshape(1, inner))


def kernel(x, mix_norm_w, ffn_norm_w, ab_w_in, ab_w_out, diff_lq1, diff_lk1, diff_lq2, diff_lk2, diff_subln_w, cd_w_in, c_conv_w, c_conv_b, c_dt_bias, c_a_log, c_d_skip, c_norm_w, d_conv_w, d_conv_b, d_wq, d_wk, d_i_bias, d_f_bias, d_norm_w, cd_w_out, ffn_w_gate_up, ffn_w_down, final_norm_w):
    batch, seq, d_model = x.shape
    depth = mix_norm_w.shape[0]
    a_width = d_model // 2
    a_heads = a_width // HEAD_DIM
    b_heads = d_model // (4 * HEAD_DIM)
    ffn_hidden = ffn_w_down.shape[1]
    c_heads = c_dt_bias.shape[1]
    c_state = 128
    c_groups = (c_conv_w.shape[2] - d_model) // (2 * c_state)
    d_heads = d_i_bias.shape[1]
    assert seq % (A_DILATIONS[-1] * CHUNK) == 0 and seq % B_TQ == 0
    assert a_heads % 2 == 0 and c_heads <= 16 and 2 * d_heads <= 8

    h = x.reshape(batch * seq, d_model)
    for layer in range(depth):
        j = layer // 2
        wgu = ffn_w_gate_up[layer].astype(BF16)
        wg, wu = wgu[:, :ffn_hidden], wgu[:, ffn_hidden:]
        wd = ffn_w_down[layer].astype(BF16)
        final_w = final_norm_w if layer == depth - 1 else None
        if layer % 2 == 0:
            lambda_init = 0.8 - 0.6 * math.exp(-0.3 * layer)
            qscale = jnp.ones((ab_w_in.shape[2],), F32)
            qscale = qscale.at[:a_width].set(HEAD_DIM ** -0.5)
            qscale = qscale.at[3 * a_width:3 * a_width + 2 * b_heads * HEAD_DIM].set(HEAD_DIM ** -0.5)
            w_in = (ab_w_in[j] * qscale).astype(BF16)
            (qkv,) = _norm_proj(h, mix_norm_w[layer], [w_in], [BF16])
            ya = _mixer_a(qkv, batch, seq, a_width)
            yb = _mixer_b(qkv, diff_lq1[j], diff_lk1[j], diff_lq2[j], diff_lk2[j], diff_subln_w[j],
                          batch, seq, 3 * a_width, b_heads, lambda_init)
            wo = ab_w_out[j].astype(BF16)
            h = _post(h, [ya, yb], [wo[:a_width], wo[a_width:]], ffn_norm_w[layer], wg, wu, wd, final_w)
        else:
            xbc_w = c_conv_w.shape[2]
            sizes = (d_model, xbc_w, c_heads, d_model, d_model, d_model, d_heads, d_heads)
            offs = [0]
            for s in sizes:
                offs.append(offs[-1] + s)
            w = cd_w_in[j]
            cols = lambda i: w[:, offs[i]:offs[i + 1]]
            i_col = 16
            wgate = jnp.zeros((d_model, LANES), F32)
            wgate = wgate.at[:, :c_heads].set(cols(2))
            wgate = wgate.at[:, i_col:i_col + d_heads].set(cols(6))
            wgate = wgate.at[:, i_col + d_heads:i_col + 2 * d_heads].set(cols(7)).astype(BF16)
            ws = [cols(0).astype(BF16), cols(1).astype(BF16), cols(3).astype(BF16), cols(4).astype(BF16),
                  cols(5).astype(BF16), wgate]
            z, xbc, u, v, o_pre, gates, gates_t = _norm_proj(
                h, mix_norm_w[layer], ws, [BF16] * 5 + [F32], w_t=wgate.T)
            yc = _ssd(xbc, z, gates, gates_t, c_conv_w[j], c_conv_b[j], c_dt_bias[j], c_a_log[j],
                      c_d_skip[j], c_norm_w[j], batch, seq, c_heads, c_groups, c_state)
            yd = _mlstm(u, v, o_pre, gates, gates_t, d_conv_w[j], d_conv_b[j],
                        _block_diag(d_wq[j], 1.0), _block_diag(d_wk[j], (d_model // d_heads) ** -0.5),
                        d_i_bias[j], d_f_bias[j], d_norm_w[j], batch, seq, d_heads, i_col)
            wo = cd_w_out[j].astype(BF16)
            h = _post(h, [yc, yd], [wo[:d_model], wo[d_model:]], ffn_norm_w[layer], wg, wu, wd, final_w)
    return h.reshape(batch, seq, d_model)
```

```python
import functools
import math

import jax
import jax.numpy as jnp
from jax import lax
from jax.experimental import pallas as pl
from jax.experimental.pallas import tpu as pltpu

F32 = jnp.float32
BF16 = jnp.bfloat16

RMS_EPS = 1e-6
HEAD_DIM = 64
LANES = 128
CHUNK = 128
A_DILATIONS = (1, 4, 16)
NEG = -1e30
VMEM_LIMIT = 56 * 1024 * 1024
ROW_TILE = 512
PROJ_COLS = 512
FFN_COLS = 768
B_TQ = 256
A_PAIRS = 2
C_UNROLL = 2
D_UNROLL = 4

_NT = (((1,), (1,)), ((), ()))
_TN = (((0,), (0,)), ((), ()))


def _cparams(sem):
    return pltpu.CompilerParams(dimension_semantics=sem, vmem_limit_bytes=VMEM_LIMIT)


def _resident(shape):
    zeros = (0,) * len(shape)
    return pl.BlockSpec(shape, lambda *_: zeros, pipeline_mode=pl.Buffered(1))


def _rms(x, w):
    return x * lax.rsqrt(jnp.mean(x * x, axis=-1, keepdims=True) + RMS_EPS) * w


def _sigmoid(x):
    return 0.5 + 0.5 * jnp.tanh(0.5 * x)


def _silu(x):
    h = 0.5 * x
    return h + h * jnp.tanh(h)


def _softplus(x):
    return jnp.maximum(x, 0.0) + jnp.log(1.0 + jnp.exp(-jnp.abs(x)))


def _tri(n, lower):
    r = lax.broadcasted_iota(jnp.int32, (n, n), 0)
    c = lax.broadcasted_iota(jnp.int32, (n, n), 1)
    return jnp.where((c <= r) if lower else (r <= c), 1.0, 0.0).astype(F32)


def _norm_proj_kernel(*refs, n_w, has_t):
    x_ref, nw_ref = refs[0], refs[1]
    w_refs = refs[2:2 + n_w]
    pos = 2 + n_w
    wt_ref = refs[pos] if has_t else None
    pos += int(has_t)
    o_refs = refs[pos:pos + n_w]
    ot_ref = refs[pos + n_w] if has_t else None

    xn = _rms(x_ref[...], nw_ref[...]).astype(BF16)
    for w_ref, o_ref in zip(w_refs, o_refs):
        n = w_ref.shape[1]
        for c0 in range(0, n, PROJ_COLS):
            c1 = min(n, c0 + PROJ_COLS)
            o_ref[:, c0:c1] = jnp.dot(xn, w_ref[:, c0:c1],
                                      preferred_element_type=F32).astype(o_ref.dtype)
    if has_t:
        ot_ref[...] = lax.dot_general(wt_ref[...], xn, _NT, preferred_element_type=F32)


def _norm_proj(x, norm_w, weights, out_dtypes, w_t=None):
    n_rows, d = x.shape
    tm = min(ROW_TILE, n_rows)
    has_t = w_t is not None
    in_specs = [pl.BlockSpec((tm, d), lambda i: (i, 0)), _resident((1, d))]
    in_specs += [_resident(w.shape) for w in weights]
    args = [x, norm_w.reshape(1, d)] + list(weights)
    out_shape = [jax.ShapeDtypeStruct((n_rows, w.shape[1]), dt) for w, dt in zip(weights, out_dtypes)]
    out_specs = [pl.BlockSpec((tm, w.shape[1]), lambda i: (i, 0)) for w in weights]
    if has_t:
        in_specs.append(_resident(w_t.shape))
        args.append(w_t)
        out_shape.append(jax.ShapeDtypeStruct((w_t.shape[0], n_rows), F32))
        out_specs.append(pl.BlockSpec((w_t.shape[0], tm), lambda i: (0, i)))
    return pl.pallas_call(
        functools.partial(_norm_proj_kernel, n_w=len(weights), has_t=has_t),
        grid=(n_rows // tm,),
        in_specs=in_specs, out_specs=out_specs, out_shape=out_shape,
        compiler_params=_cparams(("parallel",)),
        name="norm_proj",
    )(*args)


def _post_kernel(*refs, n_y, final):
    h_ref = refs[0]
    y_refs = refs[1:1 + n_y]
    wo_refs = refs[1 + n_y:1 + 2 * n_y]
    nw_ref, wg_ref, wu_ref, wd_ref = refs[1 + 2 * n_y:5 + 2 * n_y]
    pos = 5 + 2 * n_y
    fw_ref = refs[pos] if final else None
    pos += int(final)
    o_ref, hid_ref = refs[pos], refs[pos + 1]

    h1 = h_ref[...]
    for y_ref, wo_ref in zip(y_refs, wo_refs):
        h1 = h1 + jnp.dot(y_ref[...], wo_ref[...], preferred_element_type=F32)
    xn = _rms(h1, nw_ref[...]).astype(BF16)
    f = wg_ref.shape[1]
    for c0 in range(0, f, FFN_COLS):
        c1 = min(f, c0 + FFN_COLS)
        g = jnp.dot(xn, wg_ref[:, c0:c1], preferred_element_type=F32)
        u = jnp.dot(xn, wu_ref[:, c0:c1], preferred_element_type=F32)
        hid_ref[:, c0:c1] = (_silu(g) * u).astype(BF16)
    h2 = h1 + jnp.dot(hid_ref[...], wd_ref[...], preferred_element_type=F32)
    if final:
        h2 = _rms(h2, fw_ref[...])
    o_ref[...] = h2


def _post(h, ys, wos, ffn_norm_w, wg, wu, wd, final_w=None):
    n_rows, d = h.shape
    tm = min(ROW_TILE, n_rows)
    final = final_w is not None
    in_specs = [pl.BlockSpec((tm, d), lambda i: (i, 0))]
    in_specs += [pl.BlockSpec((tm, y.shape[1]), lambda i: (i, 0)) for y in ys]
    in_specs += [_resident(w.shape) for w in wos]
    in_specs += [_resident((1, d)), _resident(wg.shape), _resident(wu.shape), _resident(wd.shape)]
    args = [h] + list(ys) + list(wos) + [ffn_norm_w.reshape(1, d), wg, wu, wd]
    if final:
        in_specs.append(_resident((1, d)))
        args.append(final_w.reshape(1, d))
    return pl.pallas_call(
        functools.partial(_post_kernel, n_y=len(ys), final=final),
        grid=(n_rows // tm,),
        in_specs=in_specs,
        out_specs=pl.BlockSpec((tm, d), lambda i: (i, 0)),
        out_shape=jax.ShapeDtypeStruct((n_rows, d), F32),
        scratch_shapes=[pltpu.VMEM((tm, wg.shape[1]), BF16)],
        compiler_params=_cparams(("parallel",)),
        name="post_ffn",
    )(*args)


def _mixer_a_kernel(q_ref, k_ref, v_ref, o_ref, *scratch, seq):
    per = len(scratch) // A_PAIRS
    for p in range(A_PAIRS):
        lanes = pl.ds(p * LANES, LANES)
        _mixer_a_pair(q_ref.at[:, lanes], k_ref.at[:, lanes], v_ref.at[:, lanes], o_ref.at[:, lanes],
                      *scratch[p * per:(p + 1) * per], seq=seq)


def _mixer_a_pair(q_ref, k_ref, v_ref, o_ref, nat, f4, q4, k4, q16, k16, v1, v4, v16,
                  *stat_refs, seq):
    stats = tuple(stat_refs[3 * g:3 * g + 3] for g in range(len(A_DILATIONS)))
    d4, d16 = A_DILATIONS[1], A_DILATIONS[2]
    ratio = d16 // d4
    ts4, ts16 = seq // d4, seq // d16
    for src, dst4, dst16 in ((q_ref, q4, q16), (k_ref, k4, k16), (v_ref, v4, v16)):
        nat[...] = src[...].astype(F32)
        for r in range(d4):
            x = nat[pl.ds(r, ts4, stride=d4), :]
            f4[r * ts4:(r + 1) * ts4, :] = x
            dst4[r * ts4:(r + 1) * ts4, 0:LANES] = x.astype(BF16)
        for r in range(d16):
            x = f4[pl.ds((r % d4) * ts4 + r // d4, ts16, stride=ratio), :]
            dst16[r * ts16:(r + 1) * ts16, 0:LANES] = x.astype(BF16)
    v1[:, 0:LANES] = v_ref[...]
    for vx in (v1, v4, v16):
        vx[:, LANES:] = jnp.ones((seq, LANES), BF16)
    srcs = ((q_ref, k_ref, v1), (q4, k4, v4), (q16, k16, v16))

    lo = lax.broadcasted_iota(jnp.int32, (CHUNK, LANES), 1) < HEAD_DIM
    zero = jnp.zeros((), BF16)
    r1 = lax.broadcasted_iota(jnp.int32, (2 * CHUNK, CHUNK), 0) % CHUNK
    c1 = lax.broadcasted_iota(jnp.int32, (2 * CHUNK, CHUNK), 1)
    causal_bias = jnp.where(c1 <= r1, 0.0, NEG).astype(F32)
    r2 = lax.broadcasted_iota(jnp.int32, (2 * CHUNK, 2 * CHUNK), 0) % CHUNK
    c2 = lax.broadcasted_iota(jnp.int32, (2 * CHUNK, 2 * CHUNK), 1)
    band_bias = jnp.where((c2 >= r2) & (c2 <= r2 + CHUNK), 0.0, NEG).astype(F32)

    def attend(q, k, v, bias):
        q2 = jnp.concatenate([jnp.where(lo, q, zero), jnp.where(lo, zero, q)], axis=0)
        s = lax.dot_general(q2, k, _NT, preferred_element_type=F32) + bias
        mb = jnp.broadcast_to(jnp.max(s, axis=-1, keepdims=True), (2 * CHUNK, LANES))
        p = jnp.exp(s - jnp.concatenate([mb] * (k.shape[0] // LANES), axis=1))
        pv = jnp.dot(p.astype(BF16), v, preferred_element_type=F32)
        pick = lambda a: jnp.where(lo, a[:CHUNK], a[CHUNK:])
        return pick(mb), pick(pv[:, :LANES]), pick(pv[:, LANES:])

    def block(g, i, r):
        d = A_DILATIONS[g]
        qr, kr, vr = srcs[g]
        b0 = (r * (seq // (d * CHUNK)) + i) * CHUNK
        qs = slice(b0, b0 + CHUNK)
        if i == 0:
            res = attend(qr[qs, :], kr[qs, :], vr[qs, :], causal_bias)
        else:
            ks = slice(b0 - CHUNK, b0 + CHUNK)
            res = attend(qr[qs, :], kr[ks, :], vr[ks, :], band_bias)
        if d == d16:
            dst = pl.ds((r % d4) * ts4 + r // d4, CHUNK, stride=ratio)
        else:
            dst = qs
        for scr, val in zip(stats[g], res):
            scr[dst, :] = val

    for g, d in enumerate(A_DILATIONS):
        for r in range(d):
            for i in range(seq // (d * CHUNK)):
                block(g, i, r)

    step = 2 * CHUNK
    for c in range(seq // step):
        r, m0 = (c * step) // ts4, (c * step) % ts4
        sl = slice(c * step, (c + 1) * step)
        nat_rows = pl.ds(d4 * m0 + r, step, stride=d4)
        views = (nat_rows, sl, sl)
        ms = [st[0][v, :] for st, v in zip(stats, views)]
        mx = jnp.maximum(jnp.maximum(ms[0], ms[1]), ms[2])
        ws = [jnp.exp(m - mx) for m in ms]
        num = sum(w * st[1][v, :] for w, st, v in zip(ws, stats, views))
        den = sum(w * st[2][v, :] for w, st, v in zip(ws, stats, views))
        nat[nat_rows, :] = num / den
    o_ref[...] = nat[...].astype(o_ref.dtype)


def _mixer_a(qkv, batch, seq, a_width):
    n_steps = a_width // (A_PAIRS * LANES)
    assert a_width % (A_PAIRS * LANES) == 0
    blk = lambda off: pl.BlockSpec((seq, A_PAIRS * LANES), lambda b, p: (b, off + p))
    per_pair = ([pltpu.VMEM((seq, LANES), F32)] * 2 + [pltpu.VMEM((seq, LANES), BF16)] * 4
                + [pltpu.VMEM((seq, 2 * LANES), BF16)] * 3
                + [pltpu.VMEM((seq, LANES), F32)] * (3 * len(A_DILATIONS)))
    return pl.pallas_call(
        functools.partial(_mixer_a_kernel, seq=seq),
        grid=(batch, n_steps),
        in_specs=[blk(0), blk(n_steps), blk(2 * n_steps)],
        out_specs=pl.BlockSpec((seq, A_PAIRS * LANES), lambda b, p: (b, p)),
        out_shape=jax.ShapeDtypeStruct((batch * seq, a_width), BF16),
        scratch_shapes=per_pair * A_PAIRS,
        compiler_params=_cparams(("parallel", "parallel")),
        name="mixer_a",
    )(qkv, qkv, qkv)


def _mixer_b_kernel(q_ref, k_ref, v_ref, lq1, lk1, lq2, lk2, sw_ref, o_ref, vext, *, seq, lambda_init):
    tq = B_TQ
    lam = (jnp.exp(jnp.sum(lq1[...] * lk1[...], axis=-1, keepdims=True))
           - jnp.exp(jnp.sum(lq2[...] * lk2[...], axis=-1, keepdims=True)) + lambda_init)
    vext[:, :LANES] = v_ref[...]
    vext[:, LANES:] = jnp.ones((seq, LANES), BF16)
    lo = lax.broadcasted_iota(jnp.int32, (tq, LANES), 1) < HEAD_DIM
    diag_bias = jnp.where(lax.broadcasted_iota(jnp.int32, (2 * tq, tq), 1)
                          <= lax.broadcasted_iota(jnp.int32, (2 * tq, tq), 0) % tq, 0.0, NEG).astype(F32)
    zero = jnp.zeros((), BF16)

    for i in range(seq // tq):
        q = q_ref[i * tq:(i + 1) * tq, :]
        q2 = jnp.concatenate([jnp.where(lo, q, zero), jnp.where(lo, zero, q)], axis=0)
        k0 = i * tq
        s_d = lax.dot_general(q2, k_ref[k0:k0 + tq, :], _NT, preferred_element_type=F32) + diag_bias
        m = jnp.max(s_d, axis=-1, keepdims=True)
        if i > 0:
            s_f = lax.dot_general(q2, k_ref[0:k0, :], _NT, preferred_element_type=F32)
            m = jnp.maximum(m, jnp.max(s_f, axis=-1, keepdims=True))
        pv = jnp.dot(jnp.exp(s_d - m).astype(BF16), vext[k0:k0 + tq, :], preferred_element_type=F32)
        if i > 0:
            pv = pv + jnp.dot(jnp.exp(s_f - m).astype(BF16), vext[0:k0, :], preferred_element_type=F32)
        on = pv[:, :LANES] * (1.0 / pv[:, LANES:])
        y = on[:tq] - lam * on[tq:]
        y = _rms(y, sw_ref[...]) * (1.0 - lambda_init)
        o_ref[i * tq:(i + 1) * tq, :] = y.astype(o_ref.dtype)


def _mixer_b(qkv, lq1, lk1, lq2, lk2, subln_w, batch, seq, col0, n_heads, lambda_init):
    off = col0 // LANES
    blk = lambda o: pl.BlockSpec((seq, LANES), lambda b, h: (b, o + h))
    vec = lambda n: pl.BlockSpec((1, n), lambda b, h: (0, 0))
    return pl.pallas_call(
        functools.partial(_mixer_b_kernel, seq=seq, lambda_init=lambda_init),
        grid=(batch, n_heads),
        in_specs=[blk(off), blk(off + n_heads), blk(off + 2 * n_heads),
                  vec(HEAD_DIM), vec(HEAD_DIM), vec(HEAD_DIM), vec(HEAD_DIM), vec(LANES)],
        out_specs=pl.BlockSpec((seq, LANES), lambda b, h: (b, h)),
        out_shape=jax.ShapeDtypeStruct((batch * seq, n_heads * LANES), BF16),
        scratch_shapes=[pltpu.VMEM((seq, 2 * LANES), BF16)],
        compiler_params=_cparams(("parallel", "parallel")),
        name="mixer_b",
    )(qkv, qkv, qkv, lq1.reshape(1, -1), lk1.reshape(1, -1), lq2.reshape(1, -1),
      lk2.reshape(1, -1), subln_w.reshape(1, -1))


CONV_K = 4
HALO = 16


def _conv_silu_chunk(x_ref, cw_ref, cb_ref, c):
    t0 = pl.multiple_of(c * CHUNK, CHUNK)
    prev = x_ref[pl.ds(pl.multiple_of(jnp.maximum(t0 - HALO, 0), HALO), HALO), :].astype(F32)
    cur = x_ref[pl.ds(t0, CHUNK), :].astype(F32)
    xw = jnp.concatenate([jnp.where(c > 0, prev, 0.0), cur], axis=0)
    acc = cb_ref[...] + cw_ref[CONV_K - 1:CONV_K, :] * cur
    for j in range(CONV_K - 1):
        shifted = pltpu.roll(xw, CONV_K - 1 - j, axis=0)[HALO:, :]
        acc = acc + cw_ref[j:j + 1, :] * shifted
    return _silu(acc)


def _ssd_kernel(xbc_ref, z_ref, g_ref, gt_ref, cw_ref, cb_ref, dtb_ref, dtbt_ref, an_ref, ant_ref,
                dsk_ref, nw_ref, o_ref, state, *, seq, inner, n_heads, n_groups, d_state):
    hp = inner // n_heads
    pairs_per_group = n_heads // n_groups // 2
    state[...] = jnp.zeros_like(state)
    lo = lax.broadcasted_iota(jnp.int32, (CHUNK, LANES), 1) < hp
    zero = jnp.zeros((), BF16)
    ltri, utri = _tri(CHUNK, True), _tri(CHUNK, False)
    causal = lax.broadcasted_iota(jnp.int32, (CHUNK, CHUNK), 1) <= lax.broadcasted_iota(jnp.int32, (CHUNK, CHUNK), 0)

    def chunk(c, carry):
        t0 = pl.multiple_of(c * CHUNK, CHUNK)
        xa = _conv_silu_chunk(xbc_ref, cw_ref, cb_ref, c)
        dt = _softplus(g_ref[pl.ds(t0, CHUNK), :] + dtb_ref[...])
        dt_t = _softplus(gt_ref[0:n_heads, pl.ds(t0, CHUNK)] + dtbt_ref[...])
        acum = jnp.dot(ltri, dt * an_ref[...], preferred_element_type=F32,
                       precision=lax.Precision.HIGHEST)
        acum_t = jnp.dot(dt_t * ant_ref[...], utri, preferred_element_type=F32,
                         precision=lax.Precision.HIGHEST)
        ys = []
        for g in range(n_groups):
            bm = xa[:, inner + g * d_state:inner + (g + 1) * d_state]
            cm = xa[:, inner + (n_groups + g) * d_state:inner + (n_groups + g + 1) * d_state]
            bmb, cmb = bm.astype(BF16), cm.astype(BF16)
            cb = lax.dot_general(cmb, bmb, _NT, preferred_element_type=F32)
            for pi in range(pairs_per_group):
                pair = g * pairs_per_group + pi
                xs = xa[:, pair * LANES:(pair + 1) * LANES]
                h0 = 2 * pair
                dtp = jnp.where(lo, dt[:, h0:h0 + 1], dt[:, h0 + 1:h0 + 2])
                xdt = (xs * dtp).astype(BF16)
                xdt_h = (jnp.where(lo, xdt, zero), jnp.where(lo, zero, xdt))
                y_diag, s_loc, a_cols, a_last = 0.0, 0.0, [], []
                for h, xh in zip((h0, h0 + 1), xdt_h):
                    acol, arow = acum[:, h:h + 1], acum_t[h:h + 1, :]
                    alast = acum[CHUNK - 1:CHUNK, h:h + 1]
                    decay = jnp.exp(jnp.where(causal, acol - arow, NEG))
                    y_diag = y_diag + jnp.dot((cb * decay).astype(BF16), xh, preferred_element_type=F32)
                    bw = (bm * jnp.exp(alast - acol)).astype(BF16)
                    s_loc = s_loc + lax.dot_general(bw, xh, _TN, preferred_element_type=F32)
                    a_cols.append(acol)
                    a_last.append(alast)
                s_in = state[pair]
                y_off = jnp.dot(cmb, s_in.astype(BF16), preferred_element_type=F32)
                y = (y_diag + y_off * jnp.exp(jnp.where(lo, a_cols[0], a_cols[1]))
                     + xs * dsk_ref[:, pair * LANES:(pair + 1) * LANES])
                state[pair] = s_in * jnp.exp(jnp.where(lo, a_last[0], a_last[1])) + s_loc
                ys.append(y)
        y = jnp.concatenate(ys, axis=-1) * _silu(z_ref[pl.ds(t0, CHUNK), :].astype(F32))
        o_ref[pl.ds(t0, CHUNK), :] = _rms(y, nw_ref[...]).astype(o_ref.dtype)
        return carry

    def chunks(cg, carry):
        for u in range(C_UNROLL):
            carry = chunk(C_UNROLL * cg + u, carry)
        return carry

    lax.fori_loop(0, seq // (C_UNROLL * CHUNK), chunks, 0)


def _ssd(xbc, z, gates, gates_t, conv_w, conv_b, dt_bias, a_log, d_skip, norm_w, batch, seq,
         n_heads, n_groups, d_state):
    inner = z.shape[1]
    xw = xbc.shape[1]
    pad = LANES - n_heads
    dtb = jnp.pad(dt_bias, (0, pad))
    a_neg = jnp.pad(-jnp.exp(a_log), (0, pad))
    dsk = jnp.repeat(d_skip, inner // n_heads)
    tok = lambda w: pl.BlockSpec((seq, w), lambda b: (b, 0))
    return pl.pallas_call(
        functools.partial(_ssd_kernel, seq=seq, inner=inner, n_heads=n_heads, n_groups=n_groups,
                          d_state=d_state),
        grid=(batch,),
        in_specs=[tok(xw), tok(inner), tok(LANES), pl.BlockSpec((LANES, seq), lambda b: (0, b)),
                  _resident((CONV_K, xw)), _resident((1, xw)), _resident((1, LANES)),
                  _resident((n_heads, 1)), _resident((1, LANES)), _resident((n_heads, 1)),
                  _resident((1, inner)), _resident((1, inner))],
        out_specs=tok(inner),
        out_shape=jax.ShapeDtypeStruct((batch * seq, inner), BF16),
        scratch_shapes=[pltpu.VMEM((n_heads // 2, d_state, LANES), F32)],
        compiler_params=_cparams(("parallel",)),
        name="ssd",
    )(xbc, z, gates, gates_t, conv_w, conv_b.reshape(1, xw), dtb.reshape(1, LANES),
      dt_bias.reshape(n_heads, 1), a_neg.reshape(1, LANES), a_neg[:n_heads].reshape(n_heads, 1),
      dsk.reshape(1, inner), norm_w.reshape(1, inner))


def _mlstm_kernel(u_ref, v_ref, op_ref, g_ref, gt_ref, cw_ref, cb_ref, wq_ref, wk_ref, ib_ref, fb_ref,
                  gbt_ref, nw_ref, o_ref, c_st, n_st, m_st, *, seq, n_heads, i_col, i_row):
    inner = u_ref.shape[1]
    dh = inner // n_heads
    c_st[...] = jnp.zeros_like(c_st)
    n_st[...] = jnp.zeros_like(n_st)
    m_st[...] = jnp.zeros_like(m_st)
    ltri, utri = _tri(CHUNK, True), _tri(CHUNK, False)
    causal = lax.broadcasted_iota(jnp.int32, (CHUNK, CHUNK), 1) <= lax.broadcasted_iota(jnp.int32, (CHUNK, CHUNK), 0)
    is_f_row = lax.broadcasted_iota(jnp.int32, (2 * n_heads, CHUNK), 0) >= n_heads

    def chunk(c, carry):
        t0 = pl.multiple_of(c * CHUNK, CHUNK)
        uc = _conv_silu_chunk(u_ref, cw_ref, cb_ref, c).astype(BF16)
        q = jnp.concatenate([jnp.dot(uc[:, j * LANES:(j + 1) * LANES], wq_ref[j], preferred_element_type=F32)
                             for j in range(inner // LANES)], axis=-1)
        k = jnp.concatenate([jnp.dot(uc[:, j * LANES:(j + 1) * LANES], wk_ref[j], preferred_element_type=F32)
                             for j in range(inner // LANES)], axis=-1)
        g = g_ref[pl.ds(t0, CHUNK), :]
        ig = g + ib_ref[...]
        logf = -_softplus(-(g + fb_ref[...]))
        bcum = jnp.dot(ltri, logf, preferred_element_type=F32, precision=lax.Precision.HIGHEST)
        gt = gt_ref[i_row:i_row + 2 * n_heads, pl.ds(t0, CHUNK)] + gbt_ref[...]
        gt = jnp.where(is_f_row, -_softplus(-gt), gt)
        bcum_t = jnp.dot(gt, utri, preferred_element_type=F32, precision=lax.Precision.HIGHEST)
        vb = v_ref[pl.ds(t0, CHUNK), :]
        for h in range(n_heads):
            sl = slice(h * dh, (h + 1) * dh)
            qh, kh, vh = q[:, sl].astype(BF16), k[:, sl], vb[:, sl]
            khb = kh.astype(BF16)
            bcol = bcum[:, i_col + n_heads + h:i_col + n_heads + h + 1]
            icol = ig[:, i_col + h:i_col + h + 1]
            brow = bcum_t[n_heads + h:n_heads + h + 1, :]
            irow = gt[h:h + 1, :]
            gtot = bcol[CHUNK - 1:CHUNK, :]
            m_in = m_st[h]
            dmat = jnp.where(causal, bcol - brow + irow, NEG)
            w_end = gtot - bcol + icol
            a_loc = jnp.max(w_end, axis=0, keepdims=True)
            m_new = jnp.maximum(gtot + m_in, a_loc)
            ke = kh * jnp.exp(w_end - m_new)
            c_loc = lax.dot_general(ke.astype(BF16), vh, _TN, preferred_element_type=F32)
            n_loc = jnp.sum(ke, axis=0, keepdims=True)
            inter = bcol + m_in
            m_out = jnp.maximum(jnp.max(dmat, axis=-1, keepdims=True), inter)
            qk = lax.dot_general(qh, khb, _NT, preferred_element_type=F32)
            wqk = jnp.exp(dmat - m_out) * qk
            e_int = jnp.exp(inter - m_out)
            c_in, n_in = c_st[h], n_st[h]
            num = (jnp.dot(wqk.astype(BF16), vh, preferred_element_type=F32)
                   + jnp.dot(qh, c_in.astype(BF16), preferred_element_type=F32) * e_int)
            den = (jnp.sum(wqk, axis=-1, keepdims=True)
                   + jnp.sum(q[:, sl] * n_in, axis=-1, keepdims=True) * e_int)
            hd = num / jnp.maximum(jnp.abs(den), jnp.exp(-m_out))
            hd = _rms(hd, nw_ref[:, sl]) * _sigmoid(op_ref[pl.ds(t0, CHUNK), sl].astype(F32))
            o_ref[pl.ds(t0, CHUNK), sl] = hd.astype(o_ref.dtype)
            d_old = jnp.exp(gtot + m_in - m_new)
            c_st[h] = d_old * c_in + c_loc
            n_st[h] = d_old * n_in + n_loc
            m_st[h] = m_new
        return carry

    def chunks(cg, carry):
        for u in range(D_UNROLL):
            carry = chunk(D_UNROLL * cg + u, carry)
        return carry

    lax.fori_loop(0, seq // (D_UNROLL * CHUNK), chunks, 0)


def _block_diag(w, scale):
    n, b, _ = w.shape
    per = LANES // b
    w = (w * scale).reshape(n // per, per, b, b)
    eye = jnp.eye(per, dtype=w.dtype)
    return jnp.einsum("gpji,pq->gpjqi", w, eye).reshape(n // per, LANES, LANES).astype(BF16)


def _mlstm(u, v, o_pre, gates, gates_t, conv_w, conv_b, wq, wk, i_bias, f_bias, norm_w, batch, seq,
           n_heads, i_col):
    inner = u.shape[1]
    dh = inner // n_heads
    ib = jnp.zeros((LANES,), F32).at[i_col:i_col + n_heads].set(i_bias)
    fb = jnp.zeros((LANES,), F32).at[i_col + n_heads:i_col + 2 * n_heads].set(f_bias)
    gbt = jnp.concatenate([i_bias, f_bias]).reshape(2 * n_heads, 1)
    tok = lambda w: pl.BlockSpec((seq, w), lambda b: (b, 0))
    return pl.pallas_call(
        functools.partial(_mlstm_kernel, seq=seq, n_heads=n_heads, i_col=i_col, i_row=i_col),
        grid=(batch,),
        in_specs=[tok(inner), tok(inner), tok(inner), tok(LANES),
                  pl.BlockSpec((LANES, seq), lambda b: (0, b)),
                  _resident((CONV_K, inner)), _resident((1, inner)),
                  _resident(wq.shape), _resident(wk.shape),
                  _resident((1, LANES)), _resident((1, LANES)), _resident((2 * n_heads, 1)),
                  _resident((1, inner))],
        out_specs=tok(inner),
        out_shape=jax.ShapeDtypeStruct((batch * seq, inner), BF16),
        scratch_shapes=[pltpu.VMEM((n_heads, dh, dh), F32),
                        pltpu.VMEM((n_heads, 1, dh), F32),
                        pltpu.VMEM((n_heads, 1, 1), F32)],
        compiler_params=_cparams(("parallel",)),
        name="mlstm",
    )(u, v, o_pre, gates, gates_t, conv_w, conv_b.reshape(1, inner), wq, wk,
      ib.reshape(1, LANES), fb.reshape(1, LANES), gbt, norm_w.reshape(1, inner))


def kernel(x, mix_norm_w, ffn_norm_w, ab_w_in, ab_w_out, diff_lq1, diff_lk1, diff_lq2, diff_lk2, diff_subln_w, cd_w_in, c_conv_w, c_conv_b, c_dt_bias, c_a_log, c_d_skip, c_norm_w, d_conv_w, d_conv_b, d_wq, d_wk, d_i_bias, d_f_bias, d_norm_w, cd_w_out, ffn_w_gate_up, ffn_w_down, final_norm_w):
    batch, seq, d_model = x.shape
    depth = mix_norm_w.shape[0]
    a_width = d_model // 2
    a_heads = a_width // HEAD_DIM
    b_heads = d_model // (4 * HEAD_DIM)
    ffn_hidden = ffn_w_down.shape[1]
    c_heads = c_dt_bias.shape[1]
    c_state = 128
    c_groups = (c_conv_w.shape[2] - d_model) // (2 * c_state)
    d_heads = d_i_bias.shape[1]
    assert seq % (A_DILATIONS[-1] * CHUNK) == 0 and seq % B_TQ == 0
    assert a_heads % 2 == 0 and c_heads <= 16 and 2 * d_heads <= 8

    h = x.reshape(batch * seq, d_model)
    for layer in range(depth):
        j = layer // 2
        wgu = ffn_w_gate_up[layer].astype(BF16)
        wg, wu = wgu[:, :ffn_hidden], wgu[:, ffn_hidden:]
        wd = ffn_w_down[layer].astype(BF16)
        final_w = final_norm_w if layer == depth - 1 else None
        if layer % 2 == 0:
            lambda_init = 0.8 - 0.6 * math.exp(-0.3 * layer)
            qscale = jnp.ones((ab_w_in.shape[2],), F32)
            qscale = qscale.at[:a_width].set(HEAD_DIM ** -0.5)
            qscale = qscale.at[3 * a_width:3 * a_width + 2 * b_heads * HEAD_DIM].set(HEAD_DIM ** -0.5)
            w_in = (ab_w_in[j] * qscale).astype(BF16)
            (qkv,) = _norm_proj(h, mix_norm_w[layer], [w_in], [BF16])
            ya = _mixer_a(qkv, batch, seq, a_width)
            yb = _mixer_b(qkv, diff_lq1[j], diff_lk1[j], diff_lq2[j], diff_lk2[j], diff_subln_w[j],
                          batch, seq, 3 * a_width, b_heads, lambda_init)
            wo = ab_w_out[j].astype(BF16)
            h = _post(h, [ya, yb], [wo[:a_width], wo[a_width:]], ffn_norm_w[layer], wg, wu, wd, final_w)
        else:
            xbc_w = c_conv_w.shape[2]
            sizes = (d_model, xbc_w, c_heads, d_model, d_model, d_model, d_heads, d_heads)
            offs = [0]
            for s in sizes:
                offs.append(offs[-1] + s)
            w = cd_w_in[j]
            cols = lambda i: w[:, offs[i]:offs[i + 1]]
            i_col = 16
            wgate = jnp.zeros((d_model, LANES), F32)
            wgate = wgate.at[:, :c_heads].set(cols(2))
            wgate = wgate.at[:, i_col:i_col + d_heads].set(cols(6))
            wgate = wgate.at[:, i_col + d_heads:i_col + 2 * d_heads].set(cols(7)).astype(BF16)
            ws = [cols(0).astype(BF16), cols(1).astype(BF16), cols(3).astype(BF16), cols(4).astype(BF16),
                  cols(5).astype(BF16), wgate]
            z, xbc, u, v, o_pre, gates, gates_t = _norm_proj(
                h, mix_norm_w[layer], ws, [BF16] * 5 + [F32], w_t=wgate.T)
            yc = _ssd(xbc, z, gates, gates_t, c_conv_w[j], c_conv_b[j], c_dt_bias[j], c_a_log[j],
                      c_d_skip[j], c_norm_w[j], batch, seq, c_heads, c_groups, c_state)
            yd = _mlstm(u, v, o_pre, gates, gates_t, d_conv_w[j], d_conv_b[j],
                        _block_diag(d_wq[j], 1.0), _block_diag(d_wk[j], (d_model // d_heads) ** -0.5),
                        d_i_bias[j], d_f_bias[j], d_norm_w[j], batch, seq, d_heads, i_col)
            wo = cd_w_out[j].astype(BF16)
            h = _post(h, [yc, yd], [wo[:d_model], wo[d_model:]], ffn_norm_w[layer], wg, wu, wd, final_w)
    return h.reshape(batch, seq, d_model)
```
